```python
import jax, jax.numpy as jnp
from jax import lax
import numpy as np


D_MODEL = 4096
BATCH = 1
SEQ = 16384
DEPTH = 1
DEC_BATCH = 32
DEC_SEQ = 64
PAST_LEN = 4096

CHUNK = 64
D_RNN = D_MODEL // 2
RNN_BLOCKS = 16
RNN_BLOCK_W = D_RNN // RNN_BLOCKS
CONV_W = 4
LRU_C = 8.0
RET_HEADS = 8
RET_DK = D_MODEL // 32
RET_DV = (D_MODEL // 2) // RET_HEADS
RET_QK = RET_HEADS * RET_DK
D_RET = RET_HEADS * RET_DV
D_MIX = D_RNN + D_RET
D_IN = 2 * D_RNN + 2 * RET_QK + 2 * D_RET
IN_SPLITS = (D_RNN, 2 * D_RNN, 2 * D_RNN + RET_QK, 2 * D_RNN + 2 * RET_QK, 2 * D_RNN + 2 * RET_QK + D_RET)
ROPE_BASE = 10000.0
N_GROUPS = 4
EXPERTS_PER_GROUP = 8
N_EXPERTS = N_GROUPS * EXPERTS_PER_GROUP
TOP_K = 2
D_EXPERT = D_MODEL // 4
MOE_BLOCK = 128
D_PLE = 256
DEEPNORM_ALPHA = (2.0 * DEPTH) ** 0.25
DEEPNORM_BETA = (8.0 * DEPTH) ** -0.25
LN_EPS = 1e-5

kernel_name = 'hybrid_rglru_retention_hmoe_stream_step'


def _layernorm(x, g, b):
    xf = x.astype(jnp.float32)
    mu = jnp.mean(xf, axis=-1, keepdims=True)
    var = jnp.mean(jnp.square(xf - mu), axis=-1, keepdims=True)
    y = (xf - mu) * lax.rsqrt(var + LN_EPS)
    return (y * g.astype(jnp.float32) + b.astype(jnp.float32)).astype(x.dtype)


def _lin_combine(left, right):
    a1, b1 = left
    a2, b2 = right
    return a1 * a2, a2 * b1 + b2


def _rglru_group(gate_br, xr, conv_state, h_state, w_conv, b_conv, w_rgate, b_rgate, w_igate, b_igate, lru_lambda):
    bsz, L, _ = xr.shape
    xpad = jnp.concatenate([conv_state.astype(xr.dtype), xr], axis=1)
    xc = b_conv
    for j in range(CONV_W):
        xc = xc + xpad[:, j:j + L] * w_conv[j]
    new_conv = xpad[:, L:]
    xb = xc.reshape(bsz, L, RNN_BLOCKS, RNN_BLOCK_W)
    r = jax.nn.sigmoid(jnp.einsum('blnc,ncd->blnd', xb, w_rgate) + b_rgate).reshape(bsz, L, D_RNN)
    i = jax.nn.sigmoid(jnp.einsum('blnc,ncd->blnd', xb, w_igate) + b_igate).reshape(bsz, L, D_RNN)
    log_a = -LRU_C * r.astype(jnp.float32) * jax.nn.softplus(-lru_lambda.astype(jnp.float32))
    a = jnp.exp(log_a)
    u = jnp.sqrt(-jnp.expm1(2.0 * log_a)) * (i * xc).astype(jnp.float32)
    u = u.at[:, 0].add(a[:, 0] * h_state.astype(jnp.float32))
    _, h = lax.associative_scan(_lin_combine, (a, u), axis=1)
    y = jax.nn.gelu(gate_br) * h.astype(xr.dtype)
    return y, new_conv, h[:, -1].astype(xr.dtype)


def _rope(t, positions):
    half = t.shape[-1] // 2
    inv = ROPE_BASE ** (-jnp.arange(half, dtype=jnp.float32) / half)
    ang = positions.astype(jnp.float32)[:, None] * inv[None, :]
    cos = jnp.cos(ang)[None, :, None, :]
    sin = jnp.sin(ang)[None, :, None, :]
    tf = t.astype(jnp.float32)
    t1, t2 = tf[..., :half], tf[..., half:]
    return jnp.concatenate([t1 * cos - t2 * sin, t1 * sin + t2 * cos], axis=-1)


def _retention_group(q, k, v, g, s_state, positions):
    bsz, L, _ = q.shape
    C = min(CHUNK, L)
    nc = L // C
    q = _rope(q.reshape(bsz, L, RET_HEADS, RET_DK), positions) * (RET_DK ** -0.5)
    k = _rope(k.reshape(bsz, L, RET_HEADS, RET_DK), positions)
    v = v.reshape(bsz, L, RET_HEADS, RET_DV).astype(jnp.float32)
    log_gamma = jnp.log1p(-jnp.exp2(-5.0 - jnp.arange(RET_HEADS, dtype=jnp.float32)))
    idx = jnp.arange(C, dtype=jnp.float32)
    dmat = jnp.exp(log_gamma[:, None, None] * jnp.abs(idx[:, None] - idx[None, :]))
    q_dec = jnp.exp(log_gamma[None, :] * (idx[:, None] + 1.0))
    k_dec = jnp.exp(log_gamma[None, :] * (C - 1.0 - idx[:, None]))
    s_dec = jnp.exp(log_gamma * C)

    def chunks(t):
        return t.reshape(bsz, nc, C, RET_HEADS, t.shape[-1]).transpose(1, 0, 2, 3, 4)

    def step(s, inp):
        qc, kc, vc = inp
        scores = jnp.einsum('bnhd,bmhd->bhnm', qc, kc) * dmat
        intra = jnp.einsum('bhnm,bmhe->bnhe', scores, vc)
        cross = jnp.einsum('bnhd,bhde->bnhe', qc, s) * q_dec[None, :, :, None]
        s = s * s_dec[None, :, None, None] + jnp.einsum('bmhd,bmhe->bhde', kc * k_dec[None, :, :, None], vc)
        return s, intra + cross

    s_fin, o = lax.scan(step, s_state.astype(jnp.float32), (chunks(q), chunks(k), chunks(v)))
    o = o.transpose(1, 0, 2, 3, 4).reshape(bsz, L, RET_HEADS, RET_DV)
    mu = jnp.mean(o, axis=-1, keepdims=True)
    var = jnp.mean(jnp.square(o - mu), axis=-1, keepdims=True)
    o = (o - mu) * lax.rsqrt(var + LN_EPS)
    y = jax.nn.silu(g) * o.reshape(bsz, L, D_RET).astype(g.dtype)
    return y, s_fin.astype(g.dtype)


def _hier_moe(x, w_router_group, w_router_expert, w_gate, w_up, w_down):
    bsz, L, D = x.shape
    n = bsz * L
    xf = x.reshape(n, D)
    gl = jnp.einsum('nd,dg->ng', xf, w_router_group).astype(jnp.float32)
    gp = jax.nn.softmax(gl, axis=-1)
    g_sel = jnp.argmax(gl, axis=-1)
    g_prob = jnp.take_along_axis(gp, g_sel[:, None], axis=-1)
    el = jnp.einsum('nd,dge->nge', xf, w_router_expert).astype(jnp.float32)
    el = jnp.take_along_axis(el, g_sel[:, None, None], axis=1)[:, 0]
    top_v, top_i = lax.top_k(el, TOP_K)
    comb = g_prob * jax.nn.softmax(top_v, axis=-1)
    eid = (g_sel[:, None] * EXPERTS_PER_GROUP + top_i).reshape(-1)
    nk = n * TOP_K
    order = jnp.argsort(eid)
    sorted_e = eid[order]
    tok = order // TOP_K
    counts = jnp.bincount(eid, length=N_EXPERTS)
    starts = jnp.cumsum(counts) - counts
    pad_counts = (counts + MOE_BLOCK - 1) // MOE_BLOCK * MOE_BLOCK
    pad_end = jnp.cumsum(pad_counts)
    pad_start = pad_end - pad_counts
    dest = pad_start[sorted_e] + jnp.arange(nk, dtype=pad_end.dtype) - starts[sorted_e]
    n_blocks = -(-nk // MOE_BLOCK) + N_EXPERTS
    buf = jnp.zeros((n_blocks * MOE_BLOCK, D), x.dtype).at[dest].set(xf[tok])
    block_e = jnp.minimum(jnp.searchsorted(pad_end, jnp.arange(n_blocks, dtype=pad_end.dtype) * MOE_BLOCK, side='right'), N_EXPERTS - 1)

    def expert_block(args):
        xb, e = args
        hdn = jax.nn.silu(xb @ w_gate[e]) * (xb @ w_up[e])
        return hdn @ w_down[e]

    yb = lax.map(expert_block, (buf.reshape(n_blocks, MOE_BLOCK, D), block_e))
    y_sorted = yb.reshape(-1, D)[dest] * comb.reshape(-1)[order][:, None].astype(x.dtype)
    return jnp.zeros((n, D), x.dtype).at[tok].add(y_sorted).reshape(bsz, L, D)


def _layer(x, p, conv_state, h_state, s_state, positions, w_in, w_conv, b_conv, w_rgate, b_rgate,
           w_igate, b_igate, lru_lambda, w_out, ln1_g, ln1_b, w_router_group, w_router_expert,
           w_gate, w_up, w_down, w_ple_gate, w_ple_proj, ln2_g, ln2_b):
    z = jnp.einsum('bld,de->ble', x, w_in)
    gate_br, xr, q, k, v, g = jnp.split(z, IN_SPLITS, axis=-1)
    y_a, new_conv, new_h = _rglru_group(gate_br, xr, conv_state, h_state, w_conv, b_conv,
                                        w_rgate, b_rgate, w_igate, b_igate, lru_lambda)
    y_b, new_s = _retention_group(q, k, v, g, s_state, positions)
    mix = jnp.einsum('ble,ed->bld', jnp.concatenate([y_a, y_b], axis=-1), w_out)
    h1 = _layernorm(DEEPNORM_ALPHA * x + mix, ln1_g, ln1_b)
    ffn = _hier_moe(h1, w_router_group, w_router_expert, w_gate, w_up, w_down)
    ple = jax.nn.sigmoid(jnp.einsum('bld,de->ble', h1, w_ple_gate)) * jnp.einsum('blp,pd->bld', p, w_ple_proj)
    h2 = _layernorm(DEEPNORM_ALPHA * h1 + ffn + ple, ln2_g, ln2_b)
    return h2, new_conv, new_h, new_s


def setup_inputs(seed: int = 0) -> dict:
    key = jax.random.key(seed)
    ks = jax.random.split(key, 32)
    f32 = jnp.float32

    def nrm(k, shape, scale):
        return jax.random.normal(k, shape, f32) * scale

    col_scale = jnp.concatenate([jnp.ones((2 * D_RNN + 2 * RET_QK,), f32),
                                 jnp.full((D_RET,), DEEPNORM_BETA, f32),
                                 jnp.ones((D_RET,), f32)])
    a0 = jax.random.uniform(ks[14], (DEPTH, D_RNN), f32, 0.9, 0.999)
    s = a0 ** (1.0 / LRU_C)
    lru_lambda = jnp.log(s) - jnp.log1p(-s)
    return {
        'x_prompt': nrm(ks[0], (BATCH, SEQ, D_MODEL), 1.0),
        'x_sample': nrm(ks[1], (DEC_BATCH, DEC_SEQ, D_MODEL), 1.0),
        'state_rglru_conv': nrm(ks[2], (DEPTH, DEC_BATCH, CONV_W - 1, D_RNN), 0.5),
        'state_rglru_h': nrm(ks[3], (DEPTH, DEC_BATCH, D_RNN), 0.5),
        'state_retention': nrm(ks[4], (DEPTH, DEC_BATCH, RET_HEADS, RET_DK, RET_DV), 0.5),
        'p_prompt': nrm(ks[5], (DEPTH, BATCH, SEQ, D_PLE), 1.0),
        'p_sample': nrm(ks[6], (DEPTH, DEC_BATCH, DEC_SEQ, D_PLE), 1.0),
        'w_in': nrm(ks[7], (DEPTH, D_MODEL, D_IN), D_MODEL ** -0.5) * col_scale,
        'w_conv': nrm(ks[8], (DEPTH, CONV_W, D_RNN), CONV_W ** -0.5),
        'b_conv': nrm(ks[9], (DEPTH, D_RNN), 0.01),
        'w_rgate': nrm(ks[10], (DEPTH, RNN_BLOCKS, RNN_BLOCK_W, RNN_BLOCK_W), RNN_BLOCK_W ** -0.5),
        'b_rgate': nrm(ks[11], (DEPTH, RNN_BLOCKS, RNN_BLOCK_W), 0.01),
        'w_igate': nrm(ks[12], (DEPTH, RNN_BLOCKS, RNN_BLOCK_W, RNN_BLOCK_W), RNN_BLOCK_W ** -0.5),
        'b_igate': nrm(ks[13], (DEPTH, RNN_BLOCKS, RNN_BLOCK_W), 0.01),
        'lru_lambda': lru_lambda,
        'w_out': nrm(ks[15], (DEPTH, D_MIX, D_MODEL), D_MIX ** -0.5) * DEEPNORM_BETA,
        'ln1_g': 1.0 + nrm(ks[16], (DEPTH, D_MODEL), 0.02),
        'ln1_b': nrm(ks[17], (DEPTH, D_MODEL), 0.02),
        'w_router_group': nrm(ks[18], (DEPTH, D_MODEL, N_GROUPS), D_MODEL ** -0.5),
        'w_router_expert': nrm(ks[19], (DEPTH, D_MODEL, N_GROUPS, EXPERTS_PER_GROUP), D_MODEL ** -0.5),
        'w_gate': nrm(ks[20], (DEPTH, N_EXPERTS, D_MODEL, D_EXPERT), D_MODEL ** -0.5),
        'w_up': nrm(ks[21], (DEPTH, N_EXPERTS, D_MODEL, D_EXPERT), D_MODEL ** -0.5) * DEEPNORM_BETA,
        'w_down': nrm(ks[22], (DEPTH, N_EXPERTS, D_EXPERT, D_MODEL), D_EXPERT ** -0.5) * DEEPNORM_BETA,
        'w_ple_gate': nrm(ks[23], (DEPTH, D_MODEL, D_MODEL), D_MODEL ** -0.5),
        'w_ple_proj': nrm(ks[24], (DEPTH, D_PLE, D_MODEL), D_PLE ** -0.5) * DEEPNORM_BETA,
        'ln2_g': 1.0 + nrm(ks[25], (DEPTH, D_MODEL), 0.02),
        'ln2_b': nrm(ks[26], (DEPTH, D_MODEL), 0.02),
    }


def reference(x_prompt, x_sample, state_rglru_conv, state_rglru_h, state_retention, p_prompt, p_sample,
              w_in, w_conv, b_conv, w_rgate, b_rgate, w_igate, b_igate, lru_lambda, w_out, ln1_g, ln1_b,
              w_router_group, w_router_expert, w_gate, w_up, w_down, w_ple_gate, w_ple_proj, ln2_g, ln2_b):
    bp = x_prompt.shape[0]
    pos_p = jnp.arange(x_prompt.shape[1], dtype=jnp.int32)
    pos_s = PAST_LEN + jnp.arange(x_sample.shape[1], dtype=jnp.int32)
    xp, xs = x_prompt, x_sample
    conv_p, h_p, ret_p, conv_s, h_s, ret_s = [], [], [], [], [], []
    for i in range(DEPTH):
        params = (w_in[i], w_conv[i], b_conv[i], w_rgate[i], b_rgate[i], w_igate[i], b_igate[i],
                  lru_lambda[i], w_out[i], ln1_g[i], ln1_b[i], w_router_group[i], w_router_expert[i],
                  w_gate[i], w_up[i], w_down[i], w_ple_gate[i], w_ple_proj[i], ln2_g[i], ln2_b[i])
        zc = jnp.zeros((bp, CONV_W - 1, D_RNN), xp.dtype)
        zh = jnp.zeros((bp, D_RNN), xp.dtype)
        zs = jnp.zeros((bp, RET_HEADS, RET_DK, RET_DV), xp.dtype)
        xp, cp, hp, sp = _layer(xp, p_prompt[i], zc, zh, zs, pos_p, *params)
        xs, cs, hs, ss = _layer(xs, p_sample[i], state_rglru_conv[i], state_rglru_h[i],
                                state_retention[i], pos_s, *params)
        conv_p.append(cp); h_p.append(hp); ret_p.append(sp)
        conv_s.append(cs); h_s.append(hs); ret_s.append(ss)
    return (xp, xs, jnp.stack(conv_p), jnp.stack(h_p), jnp.stack(ret_p),
            jnp.stack(conv_s), jnp.stack(h_s), jnp.stack(ret_s))
```

```python
import functools

import numpy as np
import jax
import jax.numpy as jnp
from jax import lax
from jax.experimental import pallas as pl
from jax.experimental.pallas import tpu as pltpu

CHUNK = 64
RNN_BLOCKS = 16
CONV_W = 4
LRU_C = 8.0
RET_HEADS = 8
ROPE_BASE = 10000.0
N_GROUPS = 4
EXPERTS_PER_GROUP = 8
N_EXPERTS = N_GROUPS * EXPERTS_PER_GROUP
TOP_K = 2
PAST_LEN = 4096
LN_EPS = 1e-5

LANES = 128
V7X_VMEM_BYTES = 64 * 1024 * 1024
VMEM_LIMIT = V7X_VMEM_BYTES * 7 // 8

F32 = jnp.float32
BF16 = jnp.bfloat16


def _pick(n, cands):
    for c in cands:
        if n % c == 0:
            return c
    raise ValueError(f"no tile in {cands} divides {n}")


def _params(*sem):
    return pltpu.CompilerParams(dimension_semantics=sem, vmem_limit_bytes=VMEM_LIMIT)


def _dot(a, b):
    return jnp.dot(a, b, preferred_element_type=F32)


def _mm_kernel(a_ref, b_ref, o_ref):
    o_ref[...] = _dot(a_ref[...], b_ref[...])


def _matmul(a, b):
    m, k = a.shape
    n = b.shape[1]
    tm = _pick(m, (1024, 512, 256, 128))
    tn = _pick(n, (512, 256, 128))
    return pl.pallas_call(
        _mm_kernel,
        grid=(m // tm, n // tn),
        in_specs=[pl.BlockSpec((tm, k), lambda i, j: (i, 0)),
                  pl.BlockSpec((k, tn), lambda i, j: (0, j))],
        out_specs=pl.BlockSpec((tm, tn), lambda i, j: (i, j)),
        out_shape=jax.ShapeDtypeStruct((m, n), F32),
        compiler_params=_params("parallel", "arbitrary"),
        name="in_proj",
    )(a, b)


def _rglru_kernel(gate_ref, xr_ref, conv0_ref, h0_ref, wconv_ref, bconv_ref, wr_ref, br_ref, wi_ref, bi_ref,
                  lam_ref, y_ref, conv_out_ref, h_out_ref, xbuf, hc, a_buf, u_buf, h_buf, *, t_rows):
    j = pl.program_id(1)
    halo = CONV_W - 1
    top = 8

    @pl.when(j == 0)
    def _():
        xbuf[top - halo:top, :] = conv0_ref[0]
        hc[...] = h0_ref[0]

    xbuf[top:top + t_rows, :] = xr_ref[...]
    conv_out_ref[0] = xr_ref[t_rows - halo:t_rows, :]
    sp = jax.nn.softplus(-lam_ref[...])

    bw = wr_ref.shape[1]
    for n in range(RNN_BLOCKS):
        cs = slice(n * bw, (n + 1) * bw)
        xc = bconv_ref[:, cs]
        for w in range(CONV_W):
            xc = xc + xbuf[top - halo + w:top - halo + w + t_rows, cs] * wconv_ref[w:w + 1, cs]
        xb = xc.astype(BF16)
        r = jax.nn.sigmoid(_dot(xb, wr_ref[n]) + br_ref[:, cs])
        ig = jax.nn.sigmoid(_dot(xb, wi_ref[n]) + bi_ref[:, cs])
        log_a = -LRU_C * r * sp[:, cs]
        a = jnp.exp(log_a)
        a_buf[:, cs] = a
        u_buf[:, cs] = jnp.sqrt(1.0 - a * a) * (ig * xc)

    xbuf[top - halo:top, :] = xr_ref[t_rows - halo:t_rows, :]

    def step(t, h):
        h = a_buf[pl.ds(t, 1), :] * h + u_buf[pl.ds(t, 1), :]
        h_buf[pl.ds(t, 1), :] = h
        return h

    h_last = lax.fori_loop(0, t_rows, step, hc[...], unroll=8)
    hc[...] = h_last
    h_out_ref[0] = h_last
    y_ref[...] = (jax.nn.gelu(gate_ref[...]) * h_buf[...]).astype(y_ref.dtype)


def _rglru(z, conv0, h0, row0, n_seq, seq_len, wconv, bconv, wr, br, wi, bi, lam):
    d_rnn = wconv.shape[1]
    t_rows = _pick(seq_len, (256, 128, 64))
    nb = seq_len // t_rows
    rb0 = row0 // t_rows
    assert row0 % t_rows == 0

    def rows(col):
        return lambda s, j: (rb0 + s * nb + j, col)

    full = lambda shape: pl.BlockSpec(shape, lambda s, j: (0,) * len(shape))
    kern = functools.partial(_rglru_kernel, t_rows=t_rows)
    return pl.pallas_call(
        kern,
        grid=(n_seq, nb),
        in_specs=[pl.BlockSpec((t_rows, d_rnn), rows(0)),
                  pl.BlockSpec((t_rows, d_rnn), rows(1)),
                  pl.BlockSpec((1, CONV_W - 1, d_rnn), lambda s, j: (s, 0, 0)),
                  pl.BlockSpec((1, 1, d_rnn), lambda s, j: (s, 0, 0)),
                  full(wconv.shape), full(bconv.shape), full(wr.shape), full(br.shape),
                  full(wi.shape), full(bi.shape), full(lam.shape)],
        out_specs=[pl.BlockSpec((t_rows, d_rnn), lambda s, j: (s * nb + j, 0)),
                   pl.BlockSpec((1, CONV_W - 1, d_rnn), lambda s, j: (s, 0, 0)),
                   pl.BlockSpec((1, 1, d_rnn), lambda s, j: (s, 0, 0))],
        out_shape=[jax.ShapeDtypeStruct((n_seq * seq_len, d_rnn), BF16),
                   jax.ShapeDtypeStruct((n_seq, CONV_W - 1, d_rnn), F32),
                   jax.ShapeDtypeStruct((n_seq, 1, d_rnn), F32)],
        scratch_shapes=[pltpu.VMEM((t_rows + 8, d_rnn), F32),
                        pltpu.VMEM((1, d_rnn), F32),
                        pltpu.VMEM((t_rows, d_rnn), F32),
                        pltpu.VMEM((t_rows, d_rnn), F32),
                        pltpu.VMEM((t_rows, d_rnn), F32)],
        compiler_params=_params("arbitrary", "arbitrary"),
        name="rglru",
    )(z, z, conv0, h0, wconv, bconv, wr, br, wi, bi, lam)


def _retention_consts(dk):
    log_gamma = np.log1p(-np.exp2(-5.0 - np.arange(RET_HEADS, dtype=np.float32))).astype(np.float32)
    idx = np.arange(CHUNK, dtype=np.float32)
    dmat = np.exp(log_gamma[:, None, None] * np.abs(idx[:, None] - idx[None, :])).astype(np.float32)
    q_dec = np.exp(log_gamma[:, None] * (idx[None, :] + 1.0)).astype(np.float32)
    k_dec = np.exp(log_gamma[:, None] * (CHUNK - 1.0 - idx[None, :])).astype(np.float32)
    s_dec = np.exp(log_gamma * CHUNK).astype(np.float32)
    q_dec = np.broadcast_to(q_dec[:, :, None], (RET_HEADS, CHUNK, LANES))
    k_dec = np.broadcast_to(k_dec[:, :, None], (RET_HEADS, CHUNK, dk))
    return jnp.asarray(dmat), jnp.asarray(q_dec), jnp.asarray(k_dec), jnp.asarray(s_dec)


def _retention_kernel(sdec_ref, q_ref, k_ref, v_ref, g_ref, cos_ref, sin_ref, dmat_ref, qdec_ref, kdec_ref,
                      s0_ref, y_ref, s_out_ref, s_acc, *, dk, dv):
    c = pl.program_id(1)

    @pl.when(c == 0)
    def _():
        s_acc[...] = s0_ref[0]

    cosf = cos_ref[...]
    sinf = sin_ref[...]
    scale = dk ** -0.5

    def rope(t):
        return t * cosf + pltpu.roll(t, dk // 2, axis=1) * sinf

    for h in range(RET_HEADS):
        qh = rope(q_ref[:, h * dk:(h + 1) * dk]) * scale
        kh = rope(k_ref[:, h * dk:(h + 1) * dk])
        vb = v_ref[:, h * dv:(h + 1) * dv].astype(BF16)
        qb = qh.astype(BF16)
        kb = kh.astype(BF16)
        scores = lax.dot_general(qb, kb, (((1,), (1,)), ((), ())), preferred_element_type=F32) * dmat_ref[h]
        intra = _dot(scores.astype(BF16), vb)
        s_h = s_acc[h]
        qd = qdec_ref[h]
        cross = _dot(qb, s_h.astype(BF16)) * jnp.concatenate([qd] * (dv // LANES), axis=1)
        kd = (kh * kdec_ref[h]).astype(BF16)
        s_acc[h] = s_h * sdec_ref[h] + lax.dot_general(kd, vb, (((0,), (0,)), ((), ())),
                                                       preferred_element_type=F32)
        o = intra + cross
        mu = jnp.mean(o, axis=-1, keepdims=True)
        oc = o - mu
        var = jnp.mean(oc * oc, axis=-1, keepdims=True)
        o = oc * lax.rsqrt(var + LN_EPS)
        gh = g_ref[:, h * dv:(h + 1) * dv]
        y_ref[:, h * dv:(h + 1) * dv] = (jax.nn.silu(gh) * o).astype(y_ref.dtype)

    s_out_ref[0] = s_acc[...]


def _retention(z, s0, cosf, sinf, row0, n_seq, seq_len, cols):
    _, heads, dk, dv = s0.shape
    chunk = min(CHUNK, seq_len)
    assert chunk == CHUNK and heads == RET_HEADS
    nc = seq_len // chunk
    rb0 = row0 // chunk
    dmat, q_dec, k_dec, s_dec = _retention_consts(dk)

    def rows(col):
        return lambda s, c: (rb0 + s * nc + c, col)

    full = lambda shape: pl.BlockSpec(shape, lambda s, c: (0,) * len(shape))
    kern = functools.partial(_retention_kernel, dk=dk, dv=dv)
    return pl.pallas_call(
        kern,
        grid=(n_seq, nc),
        in_specs=[pl.BlockSpec(memory_space=pltpu.SMEM),
                  pl.BlockSpec((chunk, heads * dk), rows(cols[0])),
                  pl.BlockSpec((chunk, heads * dk), rows(cols[1])),
                  pl.BlockSpec((chunk, heads * dv), rows(cols[2])),
                  pl.BlockSpec((chunk, heads * dv), rows(cols[3])),
                  pl.BlockSpec((chunk, dk), lambda s, c: (c, 0)),
                  pl.BlockSpec((chunk, dk), lambda s, c: (c, 0)),
                  full(dmat.shape), full(q_dec.shape), full(k_dec.shape),
                  pl.BlockSpec((1, heads, dk, dv), lambda s, c: (s, 0, 0, 0))],
        out_specs=[pl.BlockSpec((chunk, heads * dv), lambda s, c: (s * nc + c, 0)),
                   pl.BlockSpec((1, heads, dk, dv), lambda s, c: (s, 0, 0, 0))],
        out_shape=[jax.ShapeDtypeStruct((n_seq * seq_len, heads * dv), BF16),
                   jax.ShapeDtypeStruct((n_seq, heads, dk, dv), F32)],
        scratch_shapes=[pltpu.VMEM((heads, dk, dv), F32)],
        compiler_params=_params("arbitrary", "arbitrary"),
        name="retention",
    )(s_dec, z, z, z, z, cosf, sinf, dmat, q_dec, k_dec, s0)


def _rope_tables(positions, dk):
    half = dk // 2
    inv = ROPE_BASE ** (-jnp.arange(half, dtype=F32) / half)
    ang = positions.astype(F32)[:, None] * inv[None, :]
    cos, sin = jnp.cos(ang), jnp.sin(ang)
    return jnp.concatenate([cos, cos], axis=1), jnp.concatenate([-sin, sin], axis=1)


def _outproj_kernel(ya_ref, yb_ref, w1_ref, w2_ref, x_ref, o_ref, *, alpha):
    mix = _dot(ya_ref[...], w1_ref[...]) + _dot(yb_ref[...], w2_ref[...])
    o_ref[...] = alpha * x_ref[...] + mix


def _outproj(ya, yb, w_out, x, alpha):
    m, ka = ya.shape
    kb = yb.shape[1]
    n = w_out.shape[1]
    tm = _pick(m, (1024, 512, 256, 128))
    tn = _pick(n, (512, 256, 128))
    assert ka == kb
    return pl.pallas_call(
        functools.partial(_outproj_kernel, alpha=alpha),
        grid=(m // tm, n // tn),
        in_specs=[pl.BlockSpec((tm, ka), lambda i, j: (i, 0)),
                  pl.BlockSpec((tm, kb), lambda i, j: (i, 0)),
                  pl.BlockSpec((ka, tn), lambda i, j: (0, j)),
                  pl.BlockSpec((kb, tn), lambda i, j: (1, j)),
                  pl.BlockSpec((tm, tn), lambda i, j: (i, j))],
        out_specs=pl.BlockSpec((tm, tn), lambda i, j: (i, j)),
        out_shape=jax.ShapeDtypeStruct((m, n), F32),
        compiler_params=_params("parallel", "arbitrary"),
        name="out_proj",
    )(ya, yb, w_out, w_out, x)


def _layernorm_rows(x, g, b):
    mu = jnp.mean(x, axis=-1, keepdims=True)
    xc = x - mu
    var = jnp.mean(xc * xc, axis=-1, keepdims=True)
    return xc * lax.rsqrt(var + LN_EPS) * g + b


def _ln_router_kernel(pre_ref, g_ref, b_ref, wr_ref, h1_ref, h1b_ref, logit_ref):
    y = _layernorm_rows(pre_ref[...], g_ref[...], b_ref[...])
    h1_ref[...] = y
    yb = y.astype(BF16)
    h1b_ref[...] = yb
    logit_ref[...] = _dot(yb, wr_ref[...])


def _ln_router(pre, g, b, w_router):
    m, d = pre.shape
    tm = _pick(m, (256, 128))
    row = lambda i: (i, 0)
    fixed = lambda i: (0, 0)
    return pl.pallas_call(
        _ln_router_kernel,
        grid=(m // tm,),
        in_specs=[pl.BlockSpec((tm, d), row), pl.BlockSpec((1, d), fixed), pl.BlockSpec((1, d), fixed),
                  pl.BlockSpec((d, LANES), fixed)],
        out_specs=[pl.BlockSpec((tm, d), row), pl.BlockSpec((tm, d), row), pl.BlockSpec((tm, LANES), row)],
        out_shape=[jax.ShapeDtypeStruct((m, d), F32), jax.ShapeDtypeStruct((m, d), BF16),
                   jax.ShapeDtypeStruct((m, LANES), F32)],
        compiler_params=_params("parallel"),
        name="ln1_router",
    )(pre, g, b, w_router)


ROUTE_EID, ROUTE_RANK, ROUTE_COMB = 0, 2, 4


def _route_kernel(logit_ref, route_ref, count_ref, carry):
    i = pl.program_id(0)
    tm = logit_ref.shape[0]

    @pl.when(i == 0)
    def _():
        carry[...] = jnp.zeros_like(carry)

    logits = logit_ref[...]
    lane = lax.broadcasted_iota(jnp.int32, logits.shape, 1)
    neg = -jnp.inf
    big = jnp.int32(2 * LANES)

    def first_argmax(vals):
        top = jnp.max(vals, axis=1, keepdims=True)
        return top, jnp.min(jnp.where(vals == top, lane, big), axis=1, keepdims=True)

    gl = jnp.where(lane < N_GROUPS, logits, neg)
    g_max, g_sel = first_argmax(gl)
    g_prob = 1.0 / jnp.sum(jnp.where(lane < N_GROUPS, jnp.exp(logits - g_max), 0.0), axis=1, keepdims=True)
    lo = N_GROUPS + g_sel * EXPERTS_PER_GROUP
    el = jnp.where((lane >= lo) & (lane < lo + EXPERTS_PER_GROUP), logits, neg)
    v1, i1 = first_argmax(el)
    v2, i2 = first_argmax(jnp.where(lane == i1, neg, el))
    e2 = jnp.exp(v2 - v1)
    p1 = 1.0 / (1.0 + e2)
    p2 = e2 / (1.0 + e2)
    eid1 = i1 - N_GROUPS
    eid2 = i2 - N_GROUPS
    oh1 = lane == eid1
    oh2 = lane == eid2
    cnt = oh1.astype(F32) + oh2.astype(F32)
    r_i = lax.broadcasted_iota(jnp.int32, (tm, tm), 0)
    c_i = lax.broadcasted_iota(jnp.int32, (tm, tm), 1)
    tri = jnp.where(c_i < r_i, 1.0, 0.0).astype(BF16)
    before = _dot(tri, cnt.astype(BF16)) + carry[...]
    rank1 = jnp.sum(jnp.where(oh1, before, 0.0), axis=1, keepdims=True)
    rank2 = jnp.sum(jnp.where(oh2, before, 0.0), axis=1, keepdims=True)
    carry[...] = carry[...] + jnp.sum(cnt, axis=0, keepdims=True)
    count_ref[...] = carry[...]

    rec = jnp.zeros(logits.shape, F32)
    for off, val in ((ROUTE_EID, eid1.astype(F32)), (ROUTE_EID + 1, eid2.astype(F32)),
                     (ROUTE_RANK, rank1), (ROUTE_RANK + 1, rank2),
                     (ROUTE_COMB, g_prob * p1), (ROUTE_COMB + 1, g_prob * p2)):
        rec = jnp.where(lane == off, val, rec)
    route_ref[...] = rec


def _route(logits):
    m = logits.shape[0]
    tm = _pick(m, (256, 128))
    return pl.pallas_call(
        _route_kernel,
        grid=(m // tm,),
        in_specs=[pl.BlockSpec((tm, LANES), lambda i: (i, 0))],
        out_specs=[pl.BlockSpec((tm, LANES), lambda i: (i, 0)), pl.BlockSpec((1, LANES), lambda i: (0, 0))],
        out_shape=[jax.ShapeDtypeStruct((m, LANES), F32), jax.ShapeDtypeStruct((1, LANES), F32)],
        scratch_shapes=[pltpu.VMEM((1, LANES), F32)],
        compiler_params=_params("arbitrary"),
        name="route",
    )(logits)


def _dispatch_kernel(pos_ref, h1_ref, xs_in_ref, xs_ref, sem, *, tm):
    del xs_in_ref
    base = pl.program_id(0) * (tm * TOP_K)

    def row_copy(r, p):
        return pltpu.make_async_copy(h1_ref.at[pl.ds(r, 1)], xs_ref.at[pl.ds(p, 1)], sem)

    def issue(r, carry):
        for k in range(TOP_K):
            row_copy(r, pos_ref[base + TOP_K * r + k]).start()
        return carry

    def drain(r, carry):
        for k in range(TOP_K):
            row_copy(r, pos_ref[base + TOP_K * r + k]).wait()
        return carry

    lax.fori_loop(0, tm, issue, 0)
    lax.fori_loop(0, tm, drain, 0)


def _dispatch(pos, h1, n_rows):
    m, d = h1.shape
    tm = _pick(m, (256, 128))
    xs0 = jnp.zeros((n_rows, d), h1.dtype)
    return pl.pallas_call(
        functools.partial(_dispatch_kernel, tm=tm),
        grid_spec=pltpu.PrefetchScalarGridSpec(
            num_scalar_prefetch=1,
            grid=(m // tm,),
            in_specs=[pl.BlockSpec((tm, d), lambda i, pos: (i, 0)),
                      pl.BlockSpec(memory_space=pl.ANY)],
            out_specs=pl.BlockSpec(memory_space=pl.ANY),
            scratch_shapes=[pltpu.SemaphoreType.DMA(())]),
        out_shape=jax.ShapeDtypeStruct((n_rows, d), h1.dtype),
        input_output_aliases={2: 0},
        compiler_params=_params("arbitrary"),
        name="dispatch",
    )(pos, h1, xs0)


def _expert_kernel(be_ref, nu_ref, xs_ref, wg_ref, wu_ref, wd_ref, y_ref):
    del be_ref
    b = pl.program_id(0)
    f = pl.program_id(1)

    @pl.when(b < nu_ref[0])
    def _():
        x = xs_ref[...].astype(BF16)
        hg = _dot(x, wg_ref[0].astype(BF16))
        hu = _dot(x, wu_ref[0].astype(BF16))
        hdn = (jax.nn.silu(hg) * hu).astype(BF16)
        part = _dot(hdn, wd_ref[0].astype(BF16))

        @pl.when(f == 0)
        def _():
            y_ref[...] = part

        @pl.when(f > 0)
        def _():
            y_ref[...] += part

    @pl.when((b >= nu_ref[0]) & (f == 0))
    def _():
        y_ref[...] = jnp.zeros_like(y_ref)


def _experts(block_e, n_used, xs, w_gate, w_up, w_down, tg):
    n_rows, d = xs.shape
    n_e, _, d_exp = w_gate.shape
    tf = _pick(d_exp, (256, 128))
    nf = d_exp // tf
    nb = n_rows // tg

    def blk(b, nu):
        return jnp.minimum(b, nu[0] - 1)

    def fch(b, f, nu):
        return jnp.where(b < nu[0], f, nf - 1)

    return pl.pallas_call(
        _expert_kernel,
        grid_spec=pltpu.PrefetchScalarGridSpec(
            num_scalar_prefetch=2,
            grid=(nb, nf),
            in_specs=[pl.BlockSpec((tg, d), lambda b, f, be, nu: (blk(b, nu), 0)),
                      pl.BlockSpec((1, d, tf), lambda b, f, be, nu: (be[b], 0, fch(b, f, nu))),
                      pl.BlockSpec((1, d, tf), lambda b, f, be, nu: (be[b], 0, fch(b, f, nu))),
                      pl.BlockSpec((1, tf, d), lambda b, f, be, nu: (be[b], fch(b, f, nu), 0))],
            out_specs=pl.BlockSpec((tg, d), lambda b, f, be, nu: (b, 0))),
        out_shape=jax.ShapeDtypeStruct((n_rows, d), F32),
        compiler_params=_params("arbitrary", "arbitrary"),
        name="experts",
    )(block_e, n_used, xs, w_gate, w_up, w_down)


def _ple_kernel(h1b_ref, wg_ref, p_ref, wp_ref, h1_ref, o_ref, *, alpha):
    gate = jax.nn.sigmoid(_dot(h1b_ref[...], wg_ref[...]))
    o_ref[...] = alpha * h1_ref[...] + gate * _dot(p_ref[...], wp_ref[...])


def _ple(h1b, w_gate, p, w_proj, h1, alpha):
    m, d = h1b.shape
    n = w_gate.shape[1]
    dp = p.shape[1]
    tm = _pick(m, (1024, 512, 256, 128))
    tn = _pick(n, (512, 256, 128))
    return pl.pallas_call(
        functools.partial(_ple_kernel, alpha=alpha),
        grid=(m // tm, n // tn),
        in_specs=[pl.BlockSpec((tm, d), lambda i, j: (i, 0)),
                  pl.BlockSpec((d, tn), lambda i, j: (0, j)),
                  pl.BlockSpec((tm, dp), lambda i, j: (i, 0)),
                  pl.BlockSpec((dp, tn), lambda i, j: (0, j)),
                  pl.BlockSpec((tm, tn), lambda i, j: (i, j))],
        out_specs=pl.BlockSpec((tm, tn), lambda i, j: (i, j)),
        out_shape=jax.ShapeDtypeStruct((m, n), F32),
        compiler_params=_params("parallel", "arbitrary"),
        name="ple",
    )(h1b, w_gate, p, w_proj, h1)


def _combine_kernel(pos_ref, pre_ref, route_ref, g_ref, b_ref, y_hbm, o_ref, ybuf, sem, *, tm):
    base = pl.program_id(0) * (tm * TOP_K)

    def row_copy(r, k, p):
        return pltpu.make_async_copy(y_hbm.at[pl.ds(p, 1)], ybuf.at[k, pl.ds(r, 1)], sem)

    def issue(r, carry):
        for k in range(TOP_K):
            row_copy(r, k, pos_ref[base + TOP_K * r + k]).start()
        return carry

    def drain(r, carry):
        for k in range(TOP_K):
            row_copy(r, k, pos_ref[base + TOP_K * r + k]).wait()
        return carry

    lax.fori_loop(0, tm, issue, 0)
    lax.fori_loop(0, tm, drain, 0)

    acc = pre_ref[...]
    for k in range(TOP_K):
        acc = acc + ybuf[k] * route_ref[:, ROUTE_COMB + k:ROUTE_COMB + k + 1]
    o_ref[...] = _layernorm_rows(acc, g_ref[...], b_ref[...])


def _combine(pos, pre, route, g, b, y):
    m, d = pre.shape
    tm = _pick(m, (256, 128))
    return pl.pallas_call(
        functools.partial(_combine_kernel, tm=tm),
        grid_spec=pltpu.PrefetchScalarGridSpec(
            num_scalar_prefetch=1,
            grid=(m // tm,),
            in_specs=[pl.BlockSpec((tm, d), lambda i, pos: (i, 0)),
                      pl.BlockSpec((tm, LANES), lambda i, pos: (i, 0)),
                      pl.BlockSpec((1, d), lambda i, pos: (0, 0)),
                      pl.BlockSpec((1, d), lambda i, pos: (0, 0)),
                      pl.BlockSpec(memory_space=pl.ANY)],
            out_specs=pl.BlockSpec((tm, d), lambda i, pos: (i, 0)),
            scratch_shapes=[pltpu.VMEM((TOP_K, tm, d), F32), pltpu.SemaphoreType.DMA(())]),
        out_shape=jax.ShapeDtypeStruct((m, d), F32),
        compiler_params=_params("arbitrary"),
        name="combine_ln2",
    )(pos, pre, route, g, b, y)


def _moe_plan(route, counts, n_tok, tg):
    eid = route[:, ROUTE_EID:ROUTE_EID + TOP_K].astype(jnp.int32)
    rank = route[:, ROUTE_RANK:ROUTE_RANK + TOP_K].astype(jnp.int32)
    cnt = counts[0, :N_EXPERTS].astype(jnp.int32)
    pad_cnt = (cnt + tg - 1) // tg * tg
    pad_end = jnp.cumsum(pad_cnt)
    pad_start = pad_end - pad_cnt
    pos = (pad_start[eid] + rank).reshape(-1)
    n_blocks = -(-(n_tok * TOP_K) // tg) + N_EXPERTS
    n_used = (pad_end[-1] // tg).astype(jnp.int32)
    blk = jnp.minimum(jnp.arange(n_blocks, dtype=jnp.int32), n_used - 1)
    block_e = jnp.minimum(jnp.searchsorted(pad_end, blk * tg, side='right'), N_EXPERTS - 1).astype(jnp.int32)
    return pos, block_e, n_used.reshape(1), n_blocks


def _layer(xs, ps, states, w, alpha):
    d_model = xs[0].shape[-1]
    d_rnn = w['w_conv'].shape[1]
    dk, dv = states[0][2].shape[-2:]
    ret_qk = RET_HEADS * dk
    shapes = [x.shape[:2] for x in xs]
    n_rows = [b * l for b, l in shapes]
    row0 = [0, n_rows[0]]
    n_tok = sum(n_rows)

    x_all = jnp.concatenate([x.reshape(-1, d_model) for x in xs], axis=0)
    p_all = jnp.concatenate([p.reshape(-1, p.shape[-1]) for p in ps], axis=0).astype(BF16)
    z = _matmul(x_all.astype(BF16), w['w_in'])

    q_col = 2 * d_rnn // ret_qk
    v_col = (2 * d_rnn + 2 * ret_qk) // (RET_HEADS * dv)
    ret_cols = (q_col, q_col + 1, v_col, v_col + 1)
    assert 2 * d_rnn % ret_qk == 0 and (2 * d_rnn + 2 * ret_qk) % (RET_HEADS * dv) == 0

    ya, yb, new_states = [], [], []
    for gi, ((bsz, seq_len), (conv0, h0, s0)) in enumerate(zip(shapes, states)):
        y_a, conv_n, h_n = _rglru(z, conv0, h0.reshape(bsz, 1, d_rnn), row0[gi], bsz, seq_len,
                                  w['w_conv'], w['b_conv'], w['w_rgate'], w['b_rgate'], w['w_igate'],
                                  w['b_igate'], w['lru_lambda'])
        start = 0 if gi == 0 else PAST_LEN
        cosf, sinf = _rope_tables(start + jnp.arange(seq_len, dtype=jnp.int32), dk)
        y_b, s_n = _retention(z, s0, cosf, sinf, row0[gi], bsz, seq_len, ret_cols)
        ya.append(y_a)
        yb.append(y_b)
        new_states.append((conv_n, h_n.reshape(bsz, d_rnn), s_n))

    pre1 = _outproj(jnp.concatenate(ya, axis=0), jnp.concatenate(yb, axis=0), w['w_out'], x_all, alpha)
    h1, h1b, logits = _ln_router(pre1, w['ln1_g'], w['ln1_b'], w['w_router'])
    route, counts = _route(logits)

    tg = 256
    pos, block_e, n_used, n_blocks = _moe_plan(route, counts, n_tok, tg)
    xs_sorted = _dispatch(pos, h1, n_blocks * tg)
    y_sorted = _experts(block_e, n_used, xs_sorted, w['w_gate'], w['w_up'], w['w_down'], tg)
    pre2 = _ple(h1b, w['w_ple_gate'], p_all, w['w_ple_proj'], h1, alpha)
    h2 = _combine(pos, pre2, route, w['ln2_g'], w['ln2_b'], y_sorted)

    outs = [h2[r0:r0 + n].reshape(b, l, d_model) for r0, n, (b, l) in zip(row0, n_rows, shapes)]
    return outs, new_states


def kernel(x_prompt, x_sample, state_rglru_conv, state_rglru_h, state_retention, p_prompt, p_sample, w_in, w_conv, b_conv, w_rgate, b_rgate, w_igate, b_igate, lru_lambda, w_out, ln1_g, ln1_b, w_router_group, w_router_expert, w_gate, w_up, w_down, w_ple_gate, w_ple_proj, ln2_g, ln2_b):
    depth = w_in.shape[0]
    alpha = (2.0 * depth) ** 0.25
    bp = x_prompt.shape[0]
    d_model = x_prompt.shape[-1]
    d_rnn = w_conv.shape[-1]
    xs = [x_prompt, x_sample]
    new = [[], []]
    for i in range(depth):
        router = jnp.concatenate([w_router_group[i], w_router_expert[i].reshape(d_model, N_EXPERTS)], axis=1)
        router = jnp.pad(router, ((0, 0), (0, LANES - router.shape[1])))
        w = dict(w_in=w_in[i].astype(BF16), w_conv=w_conv[i], b_conv=b_conv[i].reshape(1, -1),
                 w_rgate=w_rgate[i].astype(BF16), b_rgate=b_rgate[i].reshape(1, -1),
                 w_igate=w_igate[i].astype(BF16), b_igate=b_igate[i].reshape(1, -1),
                 lru_lambda=lru_lambda[i].reshape(1, -1), w_out=w_out[i].astype(BF16),
                 ln1_g=ln1_g[i].reshape(1, -1), ln1_b=ln1_b[i].reshape(1, -1), w_router=router.astype(BF16),
                 w_gate=w_gate[i], w_up=w_up[i], w_down=w_down[i], w_ple_gate=w_ple_gate[i].astype(BF16),
                 w_ple_proj=w_ple_proj[i].astype(BF16), ln2_g=ln2_g[i].reshape(1, -1),
                 ln2_b=ln2_b[i].reshape(1, -1))
        zero_states = (jnp.zeros((bp, CONV_W - 1, d_rnn), x_prompt.dtype),
                       jnp.zeros((bp, d_rnn), x_prompt.dtype),
                       jnp.zeros((bp,) + state_retention.shape[2:], x_prompt.dtype))
        states = [zero_states, (state_rglru_conv[i], state_rglru_h[i], state_retention[i])]
        xs, st = _layer(xs, [p_prompt[i], p_sample[i]], states, w, alpha)
        for gi in range(2):
            new[gi].append(st[gi])
    stack = lambda gi, k: jnp.stack([s[k] for s in new[gi]])
    return (xs[0], xs[1], stack(0, 0), stack(0, 1), stack(0, 2), stack(1, 0), stack(1, 1), stack(1, 2))
```

```python
import functools

import numpy as np
import jax
import jax.numpy as jnp
from jax import lax
from jax.experimental import pallas as pl
from jax.experimental.pallas import tpu as pltpu

CHUNK = 64
RNN_BLOCKS = 16
CONV_W = 4
LRU_C = 8.0
RET_HEADS = 8
ROPE_BASE = 10000.0
N_GROUPS = 4
EXPERTS_PER_GROUP = 8
N_EXPERTS = N_GROUPS * EXPERTS_PER_GROUP
TOP_K = 2
PAST_LEN = 4096
LN_EPS = 1e-5

LANES = 128
V7X_VMEM_BYTES = 64 * 1024 * 1024
VMEM_LIMIT = V7X_VMEM_BYTES * 7 // 8

F32 = jnp.float32
BF16 = jnp.bfloat16


def _pick(n, cands):
    for c in cands:
        if n % c == 0:
            return c
    raise ValueError(f"no tile in {cands} divides {n}")


def _params(*sem):
    return pltpu.CompilerParams(dimension_semantics=sem, vmem_limit_bytes=VMEM_LIMIT)


def _dot(a, b):
    return jnp.dot(a, b, preferred_element_type=F32)


def _group_specs(block, npb, col_of):
    prompt = pl.BlockSpec(block, lambda i, *a: (jnp.minimum(i, npb - 1), col_of(*a)))
    sample = pl.BlockSpec(block, lambda i, *a: (jnp.maximum(i - npb, 0), col_of(*a)))
    return prompt, sample


def _mm_kernel(a_ref, b_ref, o_ref):
    o_ref[...] = _dot(a_ref[...], b_ref[...])


def _matmul(a, b):
    m, k = a.shape
    n = b.shape[1]
    tm = _pick(m, (1024, 512, 256, 128))
    tn = _pick(n, (512, 256, 128))
    return pl.pallas_call(
        _mm_kernel,
        grid=(m // tm, n // tn),
        in_specs=[pl.BlockSpec((tm, k), lambda i, j: (i, 0)),
                  pl.BlockSpec((k, tn), lambda i, j: (0, j))],
        out_specs=pl.BlockSpec((tm, tn), lambda i, j: (i, j)),
        out_shape=jax.ShapeDtypeStruct((m, n), F32),
        compiler_params=_params("parallel", "arbitrary"),
        name="in_proj",
    )(a, b)


def _rglru_kernel(gate_ref, xr_ref, conv0_ref, h0_ref, wconv_ref, bconv_ref, wr_ref, br_ref, wi_ref, bi_ref,
                  lam_ref, y_ref, conv_out_ref, h_out_ref, xbuf, hc, a_buf, u_buf, h_buf, *, t_rows):
    j = pl.program_id(1)
    halo = CONV_W - 1
    top = 8

    @pl.when(j == 0)
    def _():
        xbuf[top - halo:top, :] = conv0_ref[0]
        hc[...] = h0_ref[0]

    xbuf[top:top + t_rows, :] = xr_ref[...]
    conv_out_ref[0] = xr_ref[t_rows - halo:t_rows, :]
    sp = jax.nn.softplus(-lam_ref[...])

    bw = wr_ref.shape[1]
    for n in range(RNN_BLOCKS):
        cs = slice(n * bw, (n + 1) * bw)
        xc = bconv_ref[:, cs]
        for w in range(CONV_W):
            xc = xc + xbuf[top - halo + w:top - halo + w + t_rows, cs] * wconv_ref[w:w + 1, cs]
        xb = xc.astype(BF16)
        r = jax.nn.sigmoid(_dot(xb, wr_ref[n]) + br_ref[:, cs])
        ig = jax.nn.sigmoid(_dot(xb, wi_ref[n]) + bi_ref[:, cs])
        log_a = -LRU_C * r * sp[:, cs]
        a = jnp.exp(log_a)
        a_buf[:, cs] = a
        u_buf[:, cs] = jnp.sqrt(1.0 - a * a) * (ig * xc)

    xbuf[top - halo:top, :] = xr_ref[t_rows - halo:t_rows, :]

    def step(t, h):
        h = a_buf[pl.ds(t, 1), :] * h + u_buf[pl.ds(t, 1), :]
        h_buf[pl.ds(t, 1), :] = h
        return h

    h_last = lax.fori_loop(0, t_rows, step, hc[...], unroll=8)
    hc[...] = h_last
    h_out_ref[0] = h_last
    y_ref[...] = (jax.nn.gelu(gate_ref[...]) * h_buf[...]).astype(y_ref.dtype)


def _rglru(z, conv0, h0, row0, n_seq, seq_len, wconv, bconv, wr, br, wi, bi, lam):
    d_rnn = wconv.shape[1]
    t_rows = _pick(seq_len, (256, 128, 64))
    nb = seq_len // t_rows
    rb0 = row0 // t_rows
    assert row0 % t_rows == 0

    def rows(col):
        return lambda s, j: (rb0 + s * nb + j, col)

    full = lambda shape: pl.BlockSpec(shape, lambda s, j: (0,) * len(shape))
    kern = functools.partial(_rglru_kernel, t_rows=t_rows)
    return pl.pallas_call(
        kern,
        grid=(n_seq, nb),
        in_specs=[pl.BlockSpec((t_rows, d_rnn), rows(0)),
                  pl.BlockSpec((t_rows, d_rnn), rows(1)),
                  pl.BlockSpec((1, CONV_W - 1, d_rnn), lambda s, j: (s, 0, 0)),
                  pl.BlockSpec((1, 1, d_rnn), lambda s, j: (s, 0, 0)),
                  full(wconv.shape), full(bconv.shape), full(wr.shape), full(br.shape),
                  full(wi.shape), full(bi.shape), full(lam.shape)],
        out_specs=[pl.BlockSpec((t_rows, d_rnn), lambda s, j: (s * nb + j, 0)),
                   pl.BlockSpec((1, CONV_W - 1, d_rnn), lambda s, j: (s, 0, 0)),
                   pl.BlockSpec((1, 1, d_rnn), lambda s, j: (s, 0, 0))],
        out_shape=[jax.ShapeDtypeStruct((n_seq * seq_len, d_rnn), BF16),
                   jax.ShapeDtypeStruct((n_seq, CONV_W - 1, d_rnn), F32),
                   jax.ShapeDtypeStruct((n_seq, 1, d_rnn), F32)],
        scratch_shapes=[pltpu.VMEM((t_rows + 8, d_rnn), F32),
                        pltpu.VMEM((1, d_rnn), F32),
                        pltpu.VMEM((t_rows, d_rnn), F32),
                        pltpu.VMEM((t_rows, d_rnn), F32),
                        pltpu.VMEM((t_rows, d_rnn), F32)],
        compiler_params=_params("arbitrary", "arbitrary"),
        name="rglru",
    )(z, z, conv0, h0, wconv, bconv, wr, br, wi, bi, lam)


def _retention_consts(dk):
    log_gamma = np.log1p(-np.exp2(-5.0 - np.arange(RET_HEADS, dtype=np.float32))).astype(np.float32)
    idx = np.arange(CHUNK, dtype=np.float32)
    dmat = np.exp(log_gamma[:, None, None] * np.abs(idx[:, None] - idx[None, :])).astype(np.float32)
    q_dec = np.exp(log_gamma[:, None] * (idx[None, :] + 1.0)).astype(np.float32)
    k_dec = np.exp(log_gamma[:, None] * (CHUNK - 1.0 - idx[None, :])).astype(np.float32)
    s_dec = np.exp(log_gamma * CHUNK).astype(np.float32)
    q_dec = np.broadcast_to(q_dec[:, :, None], (RET_HEADS, CHUNK, LANES))
    k_dec = np.broadcast_to(k_dec[:, :, None], (RET_HEADS, CHUNK, dk))
    return jnp.asarray(dmat), jnp.asarray(q_dec), jnp.asarray(k_dec), jnp.asarray(s_dec)


def _retention_kernel(sdec_ref, q_ref, k_ref, v_ref, g_ref, cos_ref, sin_ref, dmat_ref, qdec_ref, kdec_ref,
                      s0_ref, y_ref, s_out_ref, s_acc, *, dk, dv):
    c = pl.program_id(1)

    @pl.when(c == 0)
    def _():
        s_acc[...] = s0_ref[0]

    cosf = cos_ref[...]
    sinf = sin_ref[...]
    scale = dk ** -0.5

    def rope(t):
        return t * cosf + pltpu.roll(t, dk // 2, axis=1) * sinf

    for h in range(RET_HEADS):
        qh = rope(q_ref[:, h * dk:(h + 1) * dk]) * scale
        kh = rope(k_ref[:, h * dk:(h + 1) * dk])
        vb = v_ref[:, h * dv:(h + 1) * dv].astype(BF16)
        qb = qh.astype(BF16)
        kb = kh.astype(BF16)
        scores = lax.dot_general(qb, kb, (((1,), (1,)), ((), ())), preferred_element_type=F32) * dmat_ref[h]
        intra = _dot(scores.astype(BF16), vb)
        s_h = s_acc[h]
        qd = qdec_ref[h]
        cross = _dot(qb, s_h.astype(BF16)) * jnp.concatenate([qd] * (dv // LANES), axis=1)
        kd = (kh * kdec_ref[h]).astype(BF16)
        s_acc[h] = s_h * sdec_ref[h] + lax.dot_general(kd, vb, (((0,), (0,)), ((), ())),
                                                       preferred_element_type=F32)
        o = intra + cross
        mu = jnp.mean(o, axis=-1, keepdims=True)
        oc = o - mu
        var = jnp.mean(oc * oc, axis=-1, keepdims=True)
        o = oc * lax.rsqrt(var + LN_EPS)
        gh = g_ref[:, h * dv:(h + 1) * dv]
        y_ref[:, h * dv:(h + 1) * dv] = (jax.nn.silu(gh) * o).astype(y_ref.dtype)

    s_out_ref[0] = s_acc[...]


def _retention(z, s0, cosf, sinf, row0, n_seq, seq_len, cols):
    _, heads, dk, dv = s0.shape
    chunk = min(CHUNK, seq_len)
    assert chunk == CHUNK and heads == RET_HEADS
    nc = seq_len // chunk
    rb0 = row0 // chunk
    dmat, q_dec, k_dec, s_dec = _retention_consts(dk)

    def rows(col):
        return lambda s, c: (rb0 + s * nc + c, col)

    full = lambda shape: pl.BlockSpec(shape, lambda s, c: (0,) * len(shape))
    kern = functools.partial(_retention_kernel, dk=dk, dv=dv)
    return pl.pallas_call(
        kern,
        grid=(n_seq, nc),
        in_specs=[pl.BlockSpec(memory_space=pltpu.SMEM),
                  pl.BlockSpec((chunk, heads * dk), rows(cols[0])),
                  pl.BlockSpec((chunk, heads * dk), rows(cols[1])),
                  pl.BlockSpec((chunk, heads * dv), rows(cols[2])),
                  pl.BlockSpec((chunk, heads * dv), rows(cols[3])),
                  pl.BlockSpec((chunk, dk), lambda s, c: (c, 0)),
                  pl.BlockSpec((chunk, dk), lambda s, c: (c, 0)),
                  full(dmat.shape), full(q_dec.shape), full(k_dec.shape),
                  pl.BlockSpec((1, heads, dk, dv), lambda s, c: (s, 0, 0, 0))],
        out_specs=[pl.BlockSpec((chunk, heads * dv), lambda s, c: (s * nc + c, 0)),
                   pl.BlockSpec((1, heads, dk, dv), lambda s, c: (s, 0, 0, 0))],
        out_shape=[jax.ShapeDtypeStruct((n_seq * seq_len, heads * dv), BF16),
                   jax.ShapeDtypeStruct((n_seq, heads, dk, dv), F32)],
        scratch_shapes=[pltpu.VMEM((heads, dk, dv), F32)],
        compiler_params=_params("arbitrary", "arbitrary"),
        name="retention",
    )(s_dec, z, z, z, z, cosf, sinf, dmat, q_dec, k_dec, s0)


def _rope_tables(positions, dk):
    half = dk // 2
    inv = ROPE_BASE ** (-jnp.arange(half, dtype=F32) / half)
    ang = positions.astype(F32)[:, None] * inv[None, :]
    cos, sin = jnp.cos(ang), jnp.sin(ang)
    return jnp.concatenate([cos, cos], axis=1), jnp.concatenate([-sin, sin], axis=1)


def _outproj_kernel(yap_ref, yas_ref, ybp_ref, ybs_ref, w1_ref, w2_ref, xp_ref, xs_ref, o_ref, *, alpha, npb):
    i = pl.program_id(0)

    def emit(ya_ref, yb_ref, x_ref):
        mix = _dot(ya_ref[...], w1_ref[...]) + _dot(yb_ref[...], w2_ref[...])
        o_ref[...] = alpha * x_ref[...] + mix

    @pl.when(i < npb)
    def _():
        emit(yap_ref, ybp_ref, xp_ref)

    @pl.when(i >= npb)
    def _():
        emit(yas_ref, ybs_ref, xs_ref)


def _outproj(ya, yb, w_out, x, alpha):
    rows = [a.shape[0] for a in ya]
    ka, kb = ya[0].shape[1], yb[0].shape[1]
    n = w_out.shape[1]
    tm = _pick(np.gcd(rows[0], rows[1]), (1024, 512, 256, 128))
    tn = _pick(n, (256, 128))
    npb = rows[0] // tm
    assert ka == kb
    zero = lambda j: 0
    col = lambda j: j
    return pl.pallas_call(
        functools.partial(_outproj_kernel, alpha=alpha, npb=npb),
        grid=(sum(rows) // tm, n // tn),
        in_specs=[*_group_specs((tm, ka), npb, zero), *_group_specs((tm, kb), npb, zero),
                  pl.BlockSpec((ka, tn), lambda i, j: (0, j)),
                  pl.BlockSpec((kb, tn), lambda i, j: (1, j)),
                  *_group_specs((tm, tn), npb, col)],
        out_specs=pl.BlockSpec((tm, tn), lambda i, j: (i, j)),
        out_shape=jax.ShapeDtypeStruct((sum(rows), n), F32),
        compiler_params=_params("parallel", "arbitrary"),
        name="out_proj",
    )(ya[0], ya[1], yb[0], yb[1], w_out, w_out, x[0], x[1])


def _layernorm_rows(x, g, b):
    mu = jnp.mean(x, axis=-1, keepdims=True)
    xc = x - mu
    var = jnp.mean(xc * xc, axis=-1, keepdims=True)
    return xc * lax.rsqrt(var + LN_EPS) * g + b


def _ln_router_kernel(pre_ref, g_ref, b_ref, wr_ref, h1_ref, h1b_ref, logit_ref):
    y = _layernorm_rows(pre_ref[...], g_ref[...], b_ref[...])
    h1_ref[...] = y
    yb = y.astype(BF16)
    h1b_ref[...] = yb
    logit_ref[...] = _dot(yb, wr_ref[...])


def _ln_router(pre, g, b, w_router):
    m, d = pre.shape
    tm = _pick(m, (256, 128))
    row = lambda i: (i, 0)
    fixed = lambda i: (0, 0)
    return pl.pallas_call(
        _ln_router_kernel,
        grid=(m // tm,),
        in_specs=[pl.BlockSpec((tm, d), row), pl.BlockSpec((1, d), fixed), pl.BlockSpec((1, d), fixed),
                  pl.BlockSpec((d, LANES), fixed)],
        out_specs=[pl.BlockSpec((tm, d), row), pl.BlockSpec((tm, d), row), pl.BlockSpec((tm, LANES), row)],
        out_shape=[jax.ShapeDtypeStruct((m, d), F32), jax.ShapeDtypeStruct((m, d), BF16),
                   jax.ShapeDtypeStruct((m, LANES), F32)],
        compiler_params=_params("parallel"),
        name="ln1_router",
    )(pre, g, b, w_router)


ROUTE_EID, ROUTE_RANK, ROUTE_COMB = 0, 2, 4


def _route_kernel(logit_ref, route_ref, count_ref, carry):
    i = pl.program_id(0)
    tm = logit_ref.shape[0]

    @pl.when(i == 0)
    def _():
        carry[...] = jnp.zeros_like(carry)

    logits = logit_ref[...]
    lane = lax.broadcasted_iota(jnp.int32, logits.shape, 1)
    neg = -jnp.inf
    big = jnp.int32(2 * LANES)

    def first_argmax(vals):
        top = jnp.max(vals, axis=1, keepdims=True)
        return top, jnp.min(jnp.where(vals == top, lane, big), axis=1, keepdims=True)

    gl = jnp.where(lane < N_GROUPS, logits, neg)
    g_max, g_sel = first_argmax(gl)
    g_prob = 1.0 / jnp.sum(jnp.where(lane < N_GROUPS, jnp.exp(logits - g_max), 0.0), axis=1, keepdims=True)
    lo = N_GROUPS + g_sel * EXPERTS_PER_GROUP
    el = jnp.where((lane >= lo) & (lane < lo + EXPERTS_PER_GROUP), logits, neg)
    v1, i1 = first_argmax(el)
    v2, i2 = first_argmax(jnp.where(lane == i1, neg, el))
    e2 = jnp.exp(v2 - v1)
    p1 = 1.0 / (1.0 + e2)
    p2 = e2 / (1.0 + e2)
    eid1 = i1 - N_GROUPS
    eid2 = i2 - N_GROUPS
    oh1 = lane == eid1
    oh2 = lane == eid2
    cnt = oh1.astype(F32) + oh2.astype(F32)
    r_i = lax.broadcasted_iota(jnp.int32, (tm, tm), 0)
    c_i = lax.broadcasted_iota(jnp.int32, (tm, tm), 1)
    tri = jnp.where(c_i < r_i, 1.0, 0.0).astype(BF16)
    before = _dot(tri, cnt.astype(BF16)) + carry[...]
    rank1 = jnp.sum(jnp.where(oh1, before, 0.0), axis=1, keepdims=True)
    rank2 = jnp.sum(jnp.where(oh2, before, 0.0), axis=1, keepdims=True)
    carry[...] = carry[...] + jnp.sum(cnt, axis=0, keepdims=True)
    count_ref[...] = carry[...]

    rec = jnp.zeros(logits.shape, F32)
    for off, val in ((ROUTE_EID, eid1.astype(F32)), (ROUTE_EID + 1, eid2.astype(F32)),
                     (ROUTE_RANK, rank1), (ROUTE_RANK + 1, rank2),
                     (ROUTE_COMB, g_prob * p1), (ROUTE_COMB + 1, g_prob * p2)):
        rec = jnp.where(lane == off, val, rec)
    route_ref[...] = rec


def _route(logits):
    m = logits.shape[0]
    tm = _pick(m, (256, 128))
    return pl.pallas_call(
        _route_kernel,
        grid=(m // tm,),
        in_specs=[pl.BlockSpec((tm, LANES), lambda i: (i, 0))],
        out_specs=[pl.BlockSpec((tm, LANES), lambda i: (i, 0)), pl.BlockSpec((1, LANES), lambda i: (0, 0))],
        out_shape=[jax.ShapeDtypeStruct((m, LANES), F32), jax.ShapeDtypeStruct((1, LANES), F32)],
        scratch_shapes=[pltpu.VMEM((1, LANES), F32)],
        compiler_params=_params("arbitrary"),
        name="route",
    )(logits)


def _dispatch_kernel(pos_ref, src_ref, xs_in_ref, xs_ref, sem, *, tm):
    del xs_in_ref
    base = pl.program_id(0) * (tm * TOP_K)

    def row_copy(r, p):
        return pltpu.make_async_copy(src_ref.at[pl.ds(r, 1)], xs_ref.at[pl.ds(p, 1)], sem)

    def issue(r, carry):
        for k in range(TOP_K):
            row_copy(r, pos_ref[base + TOP_K * r + k]).start()
        return carry

    def drain(r, carry):
        for k in range(TOP_K):
            row_copy(r, pos_ref[base + TOP_K * r + k]).wait()
        return carry

    lax.fori_loop(0, tm, issue, 0, unroll=4)
    lax.fori_loop(0, tm, drain, 0)


def _dispatch(pos, src, n_rows):
    m, d = src.shape
    tm = _pick(m, (256, 128))
    xs0 = jnp.zeros((n_rows, d), src.dtype)
    return pl.pallas_call(
        functools.partial(_dispatch_kernel, tm=tm),
        grid_spec=pltpu.PrefetchScalarGridSpec(
            num_scalar_prefetch=1,
            grid=(m // tm,),
            in_specs=[pl.BlockSpec((tm, d), lambda i, pos: (i, 0)),
                      pl.BlockSpec(memory_space=pl.ANY)],
            out_specs=pl.BlockSpec(memory_space=pl.ANY),
            scratch_shapes=[pltpu.SemaphoreType.DMA(())]),
        out_shape=jax.ShapeDtypeStruct((n_rows, d), src.dtype),
        input_output_aliases={2: 0},
        compiler_params=_params("arbitrary"),
        name="dispatch",
    )(pos, src, xs0)


STEP_VALID, STEP_NEW_WEIGHTS, STEP_HAS_NEXT = 1, 2, 4
(UP_B, UP_E, UP_F, UP_NEXT_E, UP_NEXT_F, UP_OUT_B, UP_OUT_F, UP_FLAGS, UP_FIELDS) = range(9)
(DN_HDN_B, DN_E, DN_NEXT_E, DN_FLAGS, DN_FIELDS) = range(5)


def _expert_up_kernel(tab, xs_ref, wg_hbm, wu_hbm, hdn_ref, stage_g, stage_u, wg_bf, wu_bf, sems, *, tf):
    s = pl.program_id(0)
    field = lambda k: tab[s * UP_FIELDS + k]
    flags = field(UP_FLAGS)

    def weight_copies(e, f):
        cols = pl.ds(pl.multiple_of(f * tf, tf), tf)
        return (pltpu.make_async_copy(wg_hbm.at[e, :, cols], stage_g, sems.at[0]),
                pltpu.make_async_copy(wu_hbm.at[e, :, cols], stage_u, sems.at[1]))

    @pl.when(s == 0)
    def _():
        for c in weight_copies(field(UP_E), field(UP_F)):
            c.start()

    @pl.when((flags & STEP_NEW_WEIGHTS) != 0)
    def _():
        for c in weight_copies(field(UP_E), field(UP_F)):
            c.wait()
        wg_bf[...] = stage_g[...].astype(BF16)
        wu_bf[...] = stage_u[...].astype(BF16)

        @pl.when((flags & STEP_HAS_NEXT) != 0)
        def _():
            for c in weight_copies(field(UP_NEXT_E), field(UP_NEXT_F)):
                c.start()

    @pl.when((flags & STEP_VALID) != 0)
    def _():
        x = xs_ref[...].astype(BF16)
        hg = _dot(x, wg_bf[...])
        hu = _dot(x, wu_bf[...])
        hdn_ref[...] = (jax.nn.silu(hg) * hu).astype(hdn_ref.dtype)

    @pl.when((flags & STEP_VALID) == 0)
    def _():
        hdn_ref[...] = jnp.zeros_like(hdn_ref)


def _expert_down_kernel(tab, hdn_ref, wd_hbm, y_ref, stage, wd_bf, sem):
    b = pl.program_id(0)
    field = lambda k: tab[b * DN_FIELDS + k]
    flags = field(DN_FLAGS)

    def weight_copy(e):
        return pltpu.make_async_copy(wd_hbm.at[e], stage, sem)

    @pl.when(b == 0)
    def _():
        weight_copy(field(DN_E)).start()

    @pl.when((flags & STEP_NEW_WEIGHTS) != 0)
    def _():
        weight_copy(field(DN_E)).wait()
        wd_bf[...] = stage[...].astype(BF16)

        @pl.when((flags & STEP_HAS_NEXT) != 0)
        def _():
            weight_copy(field(DN_NEXT_E)).start()

    @pl.when((flags & STEP_VALID) != 0)
    def _():
        y_ref[...] = _dot(hdn_ref[...], wd_bf[...])

    @pl.when((flags & STEP_VALID) == 0)
    def _():
        y_ref[...] = jnp.zeros_like(y_ref)


def _experts(plan, xs, w_gate, w_up, w_down, tg, tf):
    n_rows, d = xs.shape
    _, _, d_exp = w_gate.shape
    nb = n_rows // tg
    n_steps = nb * (d_exp // tf)
    assert plan['up'].shape[0] == n_steps * UP_FIELDS and plan['down'].shape[0] == nb * DN_FIELDS

    hdn = pl.pallas_call(
        functools.partial(_expert_up_kernel, tf=tf),
        grid_spec=pltpu.PrefetchScalarGridSpec(
            num_scalar_prefetch=1,
            grid=(n_steps,),
            in_specs=[pl.BlockSpec((tg, d), lambda s, t: (t[s * UP_FIELDS + UP_B], 0)),
                      pl.BlockSpec(memory_space=pl.ANY),
                      pl.BlockSpec(memory_space=pl.ANY)],
            out_specs=pl.BlockSpec((tg, tf), lambda s, t: (t[s * UP_FIELDS + UP_OUT_B], t[s * UP_FIELDS + UP_OUT_F])),
            scratch_shapes=[pltpu.VMEM((d, tf), F32), pltpu.VMEM((d, tf), F32),
                            pltpu.VMEM((d, tf), BF16), pltpu.VMEM((d, tf), BF16),
                            pltpu.SemaphoreType.DMA((2,))]),
        out_shape=jax.ShapeDtypeStruct((n_rows, d_exp), BF16),
        compiler_params=_params("arbitrary"),
        name="expert_up",
    )(plan['up'], xs, w_gate, w_up)

    return pl.pallas_call(
        _expert_down_kernel,
        grid_spec=pltpu.PrefetchScalarGridSpec(
            num_scalar_prefetch=1,
            grid=(nb,),
            in_specs=[pl.BlockSpec((tg, d_exp), lambda b, t: (t[b * DN_FIELDS + DN_HDN_B], 0)),
                      pl.BlockSpec(memory_space=pl.ANY)],
            out_specs=pl.BlockSpec((tg, d), lambda b, t: (b, 0)),
            scratch_shapes=[pltpu.VMEM((d_exp, d), F32), pltpu.VMEM((d_exp, d), BF16),
                            pltpu.SemaphoreType.DMA(())]),
        out_shape=jax.ShapeDtypeStruct((n_rows, d), F32),
        compiler_params=_params("arbitrary"),
        name="expert_down",
    )(plan['down'], hdn, w_down)


def _ple_kernel(h1b_ref, wg_ref, p_ref, wp_ref, h1_ref, o_ref, *, alpha):
    gate = jax.nn.sigmoid(_dot(h1b_ref[...], wg_ref[...]))
    o_ref[...] = alpha * h1_ref[...] + gate * _dot(p_ref[...], wp_ref[...])


def _ple(h1b, w_gate, p, w_proj, h1, alpha):
    m, d = h1b.shape
    n = w_gate.shape[1]
    dp = p.shape[1]
    tm = _pick(m, (1024, 512, 256, 128))
    tn = _pick(n, (512, 256, 128))
    return pl.pallas_call(
        functools.partial(_ple_kernel, alpha=alpha),
        grid=(m // tm, n // tn),
        in_specs=[pl.BlockSpec((tm, d), lambda i, j: (i, 0)),
                  pl.BlockSpec((d, tn), lambda i, j: (0, j)),
                  pl.BlockSpec((tm, dp), lambda i, j: (i, 0)),
                  pl.BlockSpec((dp, tn), lambda i, j: (0, j)),
                  pl.BlockSpec((tm, tn), lambda i, j: (i, j))],
        out_specs=pl.BlockSpec((tm, tn), lambda i, j: (i, j)),
        out_shape=jax.ShapeDtypeStruct((m, n), F32),
        compiler_params=_params("parallel", "arbitrary"),
        name="ple",
    )(h1b, w_gate, p, w_proj, h1)


def _combine_kernel(pos_ref, pre_ref, route_ref, g_ref, b_ref, y_hbm, op_ref, os_ref, ybuf, sems, *, tm, npb):
    i = pl.program_id(0)
    n_blocks = pl.num_programs(0)
    slot = i % 2

    def row_copy(blk, buf, r, k):
        p = pos_ref[blk * (tm * TOP_K) + TOP_K * r + k]
        return pltpu.make_async_copy(y_hbm.at[pl.ds(p, 1)], ybuf.at[buf, k, pl.ds(r, 1)], sems.at[buf])

    def gather(blk, buf):
        def issue(r, carry):
            for k in range(TOP_K):
                row_copy(blk, buf, r, k).start()
            return carry
        lax.fori_loop(0, tm, issue, 0, unroll=4)

    @pl.when(i == 0)
    def _():
        gather(0, 0)

    @pl.when(i + 1 < n_blocks)
    def _():
        gather(i + 1, 1 - slot)

    def drain(r, carry):
        for k in range(TOP_K):
            row_copy(i, slot, r, k).wait()
        return carry

    lax.fori_loop(0, tm, drain, 0)

    acc = pre_ref[...]
    for k in range(TOP_K):
        acc = acc + ybuf[slot, k] * route_ref[:, ROUTE_COMB + k:ROUTE_COMB + k + 1]
    res = _layernorm_rows(acc, g_ref[...], b_ref[...])

    @pl.when(i < npb)
    def _():
        op_ref[...] = res

    @pl.when(i >= npb)
    def _():
        os_ref[...] = res


def _combine(pos, pre, route, g, b, y, rows):
    m, d = pre.shape
    tm = _pick(np.gcd(rows[0], rows[1]), (256, 128))
    npb = rows[0] // tm
    out_p, out_s = _group_specs((tm, d), npb, lambda pos: 0)
    return pl.pallas_call(
        functools.partial(_combine_kernel, tm=tm, npb=npb),
        grid_spec=pltpu.PrefetchScalarGridSpec(
            num_scalar_prefetch=1,
            grid=(m // tm,),
            in_specs=[pl.BlockSpec((tm, d), lambda i, pos: (i, 0)),
                      pl.BlockSpec((tm, LANES), lambda i, pos: (i, 0)),
                      pl.BlockSpec((1, d), lambda i, pos: (0, 0)),
                      pl.BlockSpec((1, d), lambda i, pos: (0, 0)),
                      pl.BlockSpec(memory_space=pl.ANY)],
            out_specs=[out_p, out_s],
            scratch_shapes=[pltpu.VMEM((2, TOP_K, tm, d), F32), pltpu.SemaphoreType.DMA((2,))]),
        out_shape=[jax.ShapeDtypeStruct((rows[0], d), F32), jax.ShapeDtypeStruct((rows[1], d), F32)],
        compiler_params=_params("arbitrary"),
        name="combine_ln2",
    )(pos, pre, route, g, b, y)


def _next_flagged(flag, values):
    n = flag.shape[0]
    idx = jnp.arange(n, dtype=jnp.int32)
    at_or_after = lax.cummin(jnp.where(flag, idx, n)[::-1])[::-1]
    nxt = jnp.concatenate([at_or_after[1:], jnp.full((1,), n, jnp.int32)])
    return [v[jnp.minimum(nxt, n - 1)] for v in values], nxt < n


def _moe_plan(route, counts, n_tok, tg, nf):
    i32 = jnp.int32
    eid = route[:, ROUTE_EID:ROUTE_EID + TOP_K].astype(i32)
    rank = route[:, ROUTE_RANK:ROUTE_RANK + TOP_K].astype(i32)
    cnt = counts[0, :N_EXPERTS].astype(i32)
    nblk_e = (cnt + tg - 1) // tg
    end_e = jnp.cumsum(nblk_e)
    start_e = end_e - nblk_e
    pos = (start_e[eid] * tg + rank).reshape(-1)
    nb = -(-(n_tok * TOP_K) // tg) + N_EXPERTS
    n_used = end_e[-1]
    blk = jnp.arange(nb, dtype=i32)
    e_of_blk = jnp.sum(end_e[None, :] <= blk[:, None], axis=1).astype(i32)

    run_start = jnp.concatenate([start_e, n_used[None]])
    run_len = jnp.concatenate([nblk_e, (nb - n_used)[None]])
    s = jnp.arange(nb * nf, dtype=i32)
    e_s = e_of_blk[s // nf]
    r0 = run_start[e_s]
    n = jnp.maximum(run_len[e_s], 1)
    local = s - r0 * nf
    f_s = local // n
    b_s = r0 + local % n
    valid = e_s < N_EXPERTS
    last = n_used * nf - 1
    hold = lambda t: jnp.where(valid, t, t[last])
    ob_s, of_s = b_s, f_s
    b_s, f_s, e_s = hold(b_s), hold(f_s), hold(e_s)
    prev = jnp.maximum(s - 1, 0)
    new_w = valid & ((s == 0) | (e_s != e_s[prev]) | (f_s != f_s[prev]))
    (ne_s, nf_s), has_next = _next_flagged(new_w, (e_s, f_s))
    flags = (valid * STEP_VALID + new_w * STEP_NEW_WEIGHTS + (new_w & has_next) * STEP_HAS_NEXT).astype(i32)
    up = jnp.stack([b_s, e_s, f_s, ne_s, nf_s, ob_s, of_s, flags], axis=1).reshape(-1).astype(i32)

    used = blk < n_used
    hb = jnp.minimum(blk, n_used - 1)
    e_b = e_of_blk[hb]
    new_e = used & ((blk == 0) | (e_b != e_b[jnp.maximum(blk - 1, 0)]))
    (ne_b,), has_next_b = _next_flagged(new_e, (e_b,))
    dflags = (used * STEP_VALID + new_e * STEP_NEW_WEIGHTS + (new_e & has_next_b) * STEP_HAS_NEXT).astype(i32)
    down = jnp.stack([hb, e_b, ne_b, dflags], axis=1).reshape(-1).astype(i32)
    return pos, dict(up=up, down=down), nb


def _layer(xs, ps, states, w, alpha):
    d_model = xs[0].shape[-1]
    d_rnn = w['w_conv'].shape[1]
    dk, dv = states[0][2].shape[-2:]
    ret_qk = RET_HEADS * dk
    shapes = [x.shape[:2] for x in xs]
    n_rows = [b * l for b, l in shapes]
    row0 = [0, n_rows[0]]
    n_tok = sum(n_rows)

    x2d = [x.reshape(-1, d_model) for x in xs]
    p_all = jnp.concatenate([p.reshape(-1, p.shape[-1]) for p in ps], axis=0).astype(BF16)
    z = _matmul(jnp.concatenate([x.astype(BF16) for x in x2d], axis=0), w['w_in'])

    q_col = 2 * d_rnn // ret_qk
    v_col = (2 * d_rnn + 2 * ret_qk) // (RET_HEADS * dv)
    ret_cols = (q_col, q_col + 1, v_col, v_col + 1)
    assert 2 * d_rnn % ret_qk == 0 and (2 * d_rnn + 2 * ret_qk) % (RET_HEADS * dv) == 0

    ya, yb, new_states = [], [], []
    for gi, ((bsz, seq_len), (conv0, h0, s0)) in enumerate(zip(shapes, states)):
        y_a, conv_n, h_n = _rglru(z, conv0, h0.reshape(bsz, 1, d_rnn), row0[gi], bsz, seq_len,
                                  w['w_conv'], w['b_conv'], w['w_rgate'], w['b_rgate'], w['w_igate'],
                                  w['b_igate'], w['lru_lambda'])
        start = 0 if gi == 0 else PAST_LEN
        cosf, sinf = _rope_tables(start + jnp.arange(seq_len, dtype=jnp.int32), dk)
        y_b, s_n = _retention(z, s0, cosf, sinf, row0[gi], bsz, seq_len, ret_cols)
        ya.append(y_a)
        yb.append(y_b)
        new_states.append((conv_n, h_n.reshape(bsz, d_rnn), s_n))

    pre1 = _outproj(ya, yb, w['w_out'], x2d, alpha)
    h1, h1b, logits = _ln_router(pre1, w['ln1_g'], w['ln1_b'], w['w_router'])
    route, counts = _route(logits)

    tg = 256
    d_exp = w['w_gate'].shape[2]
    tf = _pick(d_exp, (512, 256, 128))
    pos, plan, n_blocks = _moe_plan(route, counts, n_tok, tg, d_exp // tf)
    xs_sorted = _dispatch(pos, h1, n_blocks * tg)
    y_sorted = _experts(plan, xs_sorted, w['w_gate'], w['w_up'], w['w_down'], tg, tf)
    pre2 = _ple(h1b, w['w_ple_gate'], p_all, w['w_ple_proj'], h1, alpha)
    h2 = _combine(pos, pre2, route, w['ln2_g'], w['ln2_b'], y_sorted, n_rows)

    outs = [h.reshape(b, l, d_model) for h, (b, l) in zip(h2, shapes)]
    return outs, new_states


def kernel(x_prompt, x_sample, state_rglru_conv, state_rglru_h, state_retention, p_prompt, p_sample, w_in, w_conv, b_conv, w_rgate, b_rgate, w_igate, b_igate, lru_lambda, w_out, ln1_g, ln1_b, w_router_group, w_router_expert, w_gate, w_up, w_down, w_ple_gate, w_ple_proj, ln2_g, ln2_b):
    depth = w_in.shape[0]
    alpha = (2.0 * depth) ** 0.25
    bp = x_prompt.shape[0]
    d_model = x_prompt.shape[-1]
    d_rnn = w_conv.shape[-1]
    xs = [x_prompt, x_sample]
    new = [[], []]
    for i in range(depth):
        router = jnp.concatenate([w_router_group[i], w_router_expert[i].reshape(d_model, N_EXPERTS)], axis=1)
        router = jnp.pad(router, ((0, 0), (0, LANES - router.shape[1])))
        w = dict(w_in=w_in[i].astype(BF16), w_conv=w_conv[i], b_conv=b_conv[i].reshape(1, -1),
                 w_rgate=w_rgate[i].astype(BF16), b_rgate=b_rgate[i].reshape(1, -1),
                 w_igate=w_igate[i].astype(BF16), b_igate=b_igate[i].reshape(1, -1),
                 lru_lambda=lru_lambda[i].reshape(1, -1), w_out=w_out[i].astype(BF16),
                 ln1_g=ln1_g[i].reshape(1, -1), ln1_b=ln1_b[i].reshape(1, -1), w_router=router.astype(BF16),
                 w_gate=w_gate[i], w_up=w_up[i], w_down=w_down[i], w_ple_gate=w_ple_gate[i].astype(BF16),
                 w_ple_proj=w_ple_proj[i].astype(BF16), ln2_g=ln2_g[i].reshape(1, -1),
                 ln2_b=ln2_b[i].reshape(1, -1))
        zero_states = (jnp.zeros((bp, CONV_W - 1, d_rnn), x_prompt.dtype),
                       jnp.zeros((bp, d_rnn), x_prompt.dtype),
                       jnp.zeros((bp,) + state_retention.shape[2:], x_prompt.dtype))
        states = [zero_states, (state_rglru_conv[i], state_rglru_h[i], state_retention[i])]
        xs, st = _layer(xs, [p_prompt[i], p_sample[i]], states, w, alpha)
        for gi in range(2):
            new[gi].append(st[gi])
    stack = lambda gi, k: jnp.stack([s[k] for s in new[gi]])
    return (xs[0], xs[1], stack(0, 0), stack(0, 1), stack(0, 2), stack(1, 0), stack(1, 1), stack(1, 2))
```

```python
import functools

import numpy as np
import jax
import jax.numpy as jnp
from jax import lax
from jax.experimental import pallas as pl
from jax.experimental.pallas import tpu as pltpu

CHUNK = 64
RNN_BLOCKS = 16
CONV_W = 4
LRU_C = 8.0
RET_HEADS = 8
ROPE_BASE = 10000.0
N_GROUPS = 4
EXPERTS_PER_GROUP = 8
N_EXPERTS = N_GROUPS * EXPERTS_PER_GROUP
TOP_K = 2
PAST_LEN = 4096
LN_EPS = 1e-5

LANES = 128
SUBLANES = 8
V7X_VMEM_BYTES = 64 * 1024 * 1024
VMEM_LIMIT = V7X_VMEM_BYTES * 7 // 8

F32 = jnp.float32
BF16 = jnp.bfloat16


def _pick(n, cands):
    for c in cands:
        if n % c == 0:
            return c
    raise ValueError(f"no tile in {cands} divides {n}")


def _params(*sem):
    return pltpu.CompilerParams(dimension_semantics=sem, vmem_limit_bytes=VMEM_LIMIT)


def _dot(a, b):
    return jnp.dot(a, b, preferred_element_type=F32)


def _group_specs(block, npb, col_of):
    prompt = pl.BlockSpec(block, lambda i, *a: (jnp.minimum(i, npb - 1), col_of(*a)))
    sample = pl.BlockSpec(block, lambda i, *a: (jnp.maximum(i - npb, 0), col_of(*a)))
    return prompt, sample


def _inproj_kernel(xp_ref, xs_ref, w_ref, o_ref, xb, *, npb):
    i = pl.program_id(0)
    j = pl.program_id(1)

    @pl.when((j == 0) & (i < npb))
    def _():
        xb[...] = xp_ref[...].astype(BF16)

    @pl.when((j == 0) & (i >= npb))
    def _():
        xb[...] = xs_ref[...].astype(BF16)

    o_ref[...] = _dot(xb[...], w_ref[...])


def _inproj(x, w):
    rows = [a.shape[0] for a in x]
    k = x[0].shape[1]
    n = w.shape[1]
    tm = _pick(np.gcd(rows[0], rows[1]), (512, 256, 128))
    tn = _pick(n, (512, 256, 128))
    npb = rows[0] // tm
    return pl.pallas_call(
        functools.partial(_inproj_kernel, npb=npb),
        grid=(sum(rows) // tm, n // tn),
        in_specs=[*_group_specs((tm, k), npb, lambda j: 0),
                  pl.BlockSpec((k, tn), lambda i, j: (0, j))],
        out_specs=pl.BlockSpec((tm, tn), lambda i, j: (i, j)),
        out_shape=jax.ShapeDtypeStruct((sum(rows), n), F32),
        scratch_shapes=[pltpu.VMEM((tm, k), BF16)],
        compiler_params=_params("parallel", "arbitrary"),
        name="in_proj",
    )(x[0], x[1], w)


def _rglru_kernel(gate_ref, xr_ref, conv0_ref, h0_ref, wconv_ref, bconv_ref, wr_ref, br_ref, wi_ref, bi_ref,
                  lam_ref, y_ref, conv_out_ref, h_out_ref, xbuf, hc, a_buf, u_buf, h_buf, *, t_rows):
    j = pl.program_id(1)
    halo = CONV_W - 1
    top = 8

    @pl.when(j == 0)
    def _():
        xbuf[top - halo:top, :] = conv0_ref[0]
        hc[...] = h0_ref[0]

    xbuf[top:top + t_rows, :] = xr_ref[...]
    conv_out_ref[0] = xr_ref[t_rows - halo:t_rows, :]
    sp = jax.nn.softplus(-lam_ref[...])

    bw = wr_ref.shape[1]
    for n in range(RNN_BLOCKS):
        cs = slice(n * bw, (n + 1) * bw)
        xc = bconv_ref[:, cs]
        for w in range(CONV_W):
            xc = xc + xbuf[top - halo + w:top - halo + w + t_rows, cs] * wconv_ref[w:w + 1, cs]
        xb = xc.astype(BF16)
        r = jax.nn.sigmoid(_dot(xb, wr_ref[n]) + br_ref[:, cs])
        ig = jax.nn.sigmoid(_dot(xb, wi_ref[n]) + bi_ref[:, cs])
        log_a = -LRU_C * r * sp[:, cs]
        a = jnp.exp(log_a)
        a_buf[:, cs] = a
        u_buf[:, cs] = jnp.sqrt(1.0 - a * a) * (ig * xc)

    xbuf[top - halo:top, :] = xr_ref[t_rows - halo:t_rows, :]

    def step(t, h):
        h = a_buf[pl.ds(t, 1), :] * h + u_buf[pl.ds(t, 1), :]
        h_buf[pl.ds(t, 1), :] = h
        return h

    h_last = lax.fori_loop(0, t_rows, step, hc[...], unroll=8)
    hc[...] = h_last
    h_out_ref[0] = h_last
    y_ref[...] = (jax.nn.gelu(gate_ref[...]) * h_buf[...]).astype(y_ref.dtype)


def _rglru(z, conv0, h0, row0, n_seq, seq_len, wconv, bconv, wr, br, wi, bi, lam):
    d_rnn = wconv.shape[1]
    t_rows = _pick(seq_len, (256, 128, 64))
    nb = seq_len // t_rows
    rb0 = row0 // t_rows
    assert row0 % t_rows == 0

    def rows(col):
        return lambda s, j: (rb0 + s * nb + j, col)

    full = lambda shape: pl.BlockSpec(shape, lambda s, j: (0,) * len(shape))
    kern = functools.partial(_rglru_kernel, t_rows=t_rows)
    return pl.pallas_call(
        kern,
        grid=(n_seq, nb),
        in_specs=[pl.BlockSpec((t_rows, d_rnn), rows(0)),
                  pl.BlockSpec((t_rows, d_rnn), rows(1)),
                  pl.BlockSpec((1, CONV_W - 1, d_rnn), lambda s, j: (s, 0, 0)),
                  pl.BlockSpec((1, 1, d_rnn), lambda s, j: (s, 0, 0)),
                  full(wconv.shape), full(bconv.shape), full(wr.shape), full(br.shape),
                  full(wi.shape), full(bi.shape), full(lam.shape)],
        out_specs=[pl.BlockSpec((t_rows, d_rnn), lambda s, j: (s * nb + j, 0)),
                   pl.BlockSpec((1, CONV_W - 1, d_rnn), lambda s, j: (s, 0, 0)),
                   pl.BlockSpec((1, 1, d_rnn), lambda s, j: (s, 0, 0))],
        out_shape=[jax.ShapeDtypeStruct((n_seq * seq_len, d_rnn), BF16),
                   jax.ShapeDtypeStruct((n_seq, CONV_W - 1, d_rnn), F32),
                   jax.ShapeDtypeStruct((n_seq, 1, d_rnn), F32)],
        scratch_shapes=[pltpu.VMEM((t_rows + 8, d_rnn), F32),
                        pltpu.VMEM((1, d_rnn), F32),
                        pltpu.VMEM((t_rows, d_rnn), F32),
                        pltpu.VMEM((t_rows, d_rnn), F32),
                        pltpu.VMEM((t_rows, d_rnn), F32)],
        compiler_params=_params("arbitrary", "arbitrary"),
        name="rglru",
    )(z, z, conv0, h0, wconv, bconv, wr, br, wi, bi, lam)


RET_CHUNKS_PER_STEP = 4


def _retention_consts(dk):
    log_gamma = np.log1p(-np.exp2(-5.0 - np.arange(RET_HEADS, dtype=np.float32))).astype(np.float32)
    idx = np.arange(CHUNK, dtype=np.float32)
    dmat = np.exp(log_gamma[:, None, None] * np.abs(idx[:, None] - idx[None, :])).astype(np.float32)
    q_dec = np.exp(log_gamma[:, None] * (idx[None, :] + 1.0)).astype(np.float32)
    k_dec = np.exp(log_gamma[:, None] * (CHUNK - 1.0 - idx[None, :])).astype(np.float32)
    s_dec = np.exp(log_gamma * CHUNK).astype(np.float32)
    q_dec = np.broadcast_to(q_dec[:, :, None], (RET_HEADS, CHUNK, LANES))
    k_dec = np.broadcast_to(k_dec[:, :, None], (RET_HEADS, CHUNK, dk))
    return jnp.asarray(dmat), jnp.asarray(q_dec), jnp.asarray(k_dec), jnp.asarray(s_dec)


def _retention_kernel(sdec_ref, q_ref, k_ref, v_ref, g_ref, cos_ref, sin_ref, dmat_ref, qdec_ref, kdec_ref,
                      s0_ref, y_ref, s_out_ref, s_acc, *, dk, dv, chunks):
    c = pl.program_id(1)

    @pl.when(c == 0)
    def _():
        s_acc[...] = s0_ref[0]

    scale = dk ** -0.5

    for ci in range(chunks):
        rows = slice(ci * CHUNK, (ci + 1) * CHUNK)
        cosf = cos_ref[rows, :]
        sinf = sin_ref[rows, :]

        def rope(t):
            return t * cosf + pltpu.roll(t, dk // 2, axis=1) * sinf

        for h in range(RET_HEADS):
            qh = rope(q_ref[rows, h * dk:(h + 1) * dk]) * scale
            kh = rope(k_ref[rows, h * dk:(h + 1) * dk])
            vb = v_ref[rows, h * dv:(h + 1) * dv].astype(BF16)
            qb = qh.astype(BF16)
            kb = kh.astype(BF16)
            scores = lax.dot_general(qb, kb, (((1,), (1,)), ((), ())), preferred_element_type=F32) * dmat_ref[h]
            intra = _dot(scores.astype(BF16), vb)
            s_h = s_acc[h]
            qd = qdec_ref[h]
            cross = _dot(qb, s_h.astype(BF16)) * jnp.concatenate([qd] * (dv // LANES), axis=1)
            kd = (kh * kdec_ref[h]).astype(BF16)
            s_acc[h] = s_h * sdec_ref[h] + lax.dot_general(kd, vb, (((0,), (0,)), ((), ())),
                                                           preferred_element_type=F32)
            o = intra + cross
            mu = jnp.mean(o, axis=-1, keepdims=True)
            oc = o - mu
            var = jnp.mean(oc * oc, axis=-1, keepdims=True)
            o = oc * lax.rsqrt(var + LN_EPS)
            gh = g_ref[rows, h * dv:(h + 1) * dv]
            y_ref[rows, h * dv:(h + 1) * dv] = (jax.nn.silu(gh) * o).astype(y_ref.dtype)

    s_out_ref[0] = s_acc[...]


def _retention(z, s0, cosf, sinf, row0, n_seq, seq_len, cols):
    _, heads, dk, dv = s0.shape
    chunk = min(CHUNK, seq_len)
    assert chunk == CHUNK and heads == RET_HEADS
    chunks = _pick(seq_len // chunk, (RET_CHUNKS_PER_STEP, 1))
    t_rows = chunks * chunk
    nc = seq_len // t_rows
    rb0 = row0 // t_rows
    assert row0 % t_rows == 0
    dmat, q_dec, k_dec, s_dec = _retention_consts(dk)

    def rows(col):
        return lambda s, c: (rb0 + s * nc + c, col)

    full = lambda shape: pl.BlockSpec(shape, lambda s, c: (0,) * len(shape))
    kern = functools.partial(_retention_kernel, dk=dk, dv=dv, chunks=chunks)
    return pl.pallas_call(
        kern,
        grid=(n_seq, nc),
        in_specs=[pl.BlockSpec(memory_space=pltpu.SMEM),
                  pl.BlockSpec((t_rows, heads * dk), rows(cols[0])),
                  pl.BlockSpec((t_rows, heads * dk), rows(cols[1])),
                  pl.BlockSpec((t_rows, heads * dv), rows(cols[2])),
                  pl.BlockSpec((t_rows, heads * dv), rows(cols[3])),
                  pl.BlockSpec((t_rows, dk), lambda s, c: (c, 0)),
                  pl.BlockSpec((t_rows, dk), lambda s, c: (c, 0)),
                  full(dmat.shape), full(q_dec.shape), full(k_dec.shape),
                  pl.BlockSpec((1, heads, dk, dv), lambda s, c: (s, 0, 0, 0))],
        out_specs=[pl.BlockSpec((t_rows, heads * dv), lambda s, c: (s * nc + c, 0)),
                   pl.BlockSpec((1, heads, dk, dv), lambda s, c: (s, 0, 0, 0))],
        out_shape=[jax.ShapeDtypeStruct((n_seq * seq_len, heads * dv), BF16),
                   jax.ShapeDtypeStruct((n_seq, heads, dk, dv), F32)],
        scratch_shapes=[pltpu.VMEM((heads, dk, dv), F32)],
        compiler_params=_params("arbitrary", "arbitrary"),
        name="retention",
    )(s_dec, z, z, z, z, cosf, sinf, dmat, q_dec, k_dec, s0)


def _rope_tables(positions, dk):
    half = dk // 2
    inv = ROPE_BASE ** (-jnp.arange(half, dtype=F32) / half)
    ang = positions.astype(F32)[:, None] * inv[None, :]
    cos, sin = jnp.cos(ang), jnp.sin(ang)
    return jnp.concatenate([cos, cos], axis=1), jnp.concatenate([-sin, sin], axis=1)


def _outproj_kernel(yap_ref, yas_ref, ybp_ref, ybs_ref, w1_ref, w2_ref, xp_ref, xs_ref, o_ref, *, alpha, npb):
    i = pl.program_id(0)

    def emit(ya_ref, yb_ref, x_ref):
        mix = _dot(ya_ref[...], w1_ref[...]) + _dot(yb_ref[...], w2_ref[...])
        o_ref[...] = alpha * x_ref[...] + mix

    @pl.when(i < npb)
    def _():
        emit(yap_ref, ybp_ref, xp_ref)

    @pl.when(i >= npb)
    def _():
        emit(yas_ref, ybs_ref, xs_ref)


def _outproj(ya, yb, w_out, x, alpha):
    rows = [a.shape[0] for a in ya]
    ka, kb = ya[0].shape[1], yb[0].shape[1]
    n = w_out.shape[1]
    tm = _pick(np.gcd(rows[0], rows[1]), (1024, 512, 256, 128))
    tn = _pick(n, (256, 128))
    npb = rows[0] // tm
    assert ka == kb
    zero = lambda j: 0
    col = lambda j: j
    return pl.pallas_call(
        functools.partial(_outproj_kernel, alpha=alpha, npb=npb),
        grid=(sum(rows) // tm, n // tn),
        in_specs=[*_group_specs((tm, ka), npb, zero), *_group_specs((tm, kb), npb, zero),
                  pl.BlockSpec((ka, tn), lambda i, j: (0, j)),
                  pl.BlockSpec((kb, tn), lambda i, j: (1, j)),
                  *_group_specs((tm, tn), npb, col)],
        out_specs=pl.BlockSpec((tm, tn), lambda i, j: (i, j)),
        out_shape=jax.ShapeDtypeStruct((sum(rows), n), F32),
        compiler_params=_params("parallel", "arbitrary"),
        name="out_proj",
    )(ya[0], ya[1], yb[0], yb[1], w_out, w_out, x[0], x[1])


def _layernorm_rows(x, g, b):
    mu = jnp.mean(x, axis=-1, keepdims=True)
    xc = x - mu
    var = jnp.mean(xc * xc, axis=-1, keepdims=True)
    return xc * lax.rsqrt(var + LN_EPS) * g + b


def _ln_router_kernel(pre_ref, g_ref, b_ref, wr_ref, h1_ref, h1b_ref, logit_ref):
    y = _layernorm_rows(pre_ref[...], g_ref[...], b_ref[...])
    h1_ref[...] = y
    yb = y.astype(BF16)
    h1b_ref[...] = yb
    logit_ref[...] = _dot(yb, wr_ref[...])


def _ln_router(pre, g, b, w_router):
    m, d = pre.shape
    tm = _pick(m, (256, 128))
    row = lambda i: (i, 0)
    fixed = lambda i: (0, 0)
    return pl.pallas_call(
        _ln_router_kernel,
        grid=(m // tm,),
        in_specs=[pl.BlockSpec((tm, d), row), pl.BlockSpec((1, d), fixed), pl.BlockSpec((1, d), fixed),
                  pl.BlockSpec((d, LANES), fixed)],
        out_specs=[pl.BlockSpec((tm, d), row), pl.BlockSpec((tm, d), row), pl.BlockSpec((tm, LANES), row)],
        out_shape=[jax.ShapeDtypeStruct((m, d), F32), jax.ShapeDtypeStruct((m, d), BF16),
                   jax.ShapeDtypeStruct((m, LANES), F32)],
        compiler_params=_params("parallel"),
        name="ln1_router",
    )(pre, g, b, w_router)


ROUTE_EID, ROUTE_RANK, ROUTE_COMB = 0, 2, 4


def _route_kernel(logit_ref, route_ref, count_ref, carry):
    i = pl.program_id(0)
    tm = logit_ref.shape[0]

    @pl.when(i == 0)
    def _():
        carry[...] = jnp.zeros_like(carry)

    logits = logit_ref[...]
    lane = lax.broadcasted_iota(jnp.int32, logits.shape, 1)
    neg = -jnp.inf
    big = jnp.int32(2 * LANES)

    def first_argmax(vals):
        top = jnp.max(vals, axis=1, keepdims=True)
        return top, jnp.min(jnp.where(vals == top, lane, big), axis=1, keepdims=True)

    gl = jnp.where(lane < N_GROUPS, logits, neg)
    g_max, g_sel = first_argmax(gl)
    g_prob = 1.0 / jnp.sum(jnp.where(lane < N_GROUPS, jnp.exp(logits - g_max), 0.0), axis=1, keepdims=True)
    lo = N_GROUPS + g_sel * EXPERTS_PER_GROUP
    el = jnp.where((lane >= lo) & (lane < lo + EXPERTS_PER_GROUP), logits, neg)
    v1, i1 = first_argmax(el)
    v2, i2 = first_argmax(jnp.where(lane == i1, neg, el))
    e2 = jnp.exp(v2 - v1)
    p1 = 1.0 / (1.0 + e2)
    p2 = e2 / (1.0 + e2)
    eid1 = i1 - N_GROUPS
    eid2 = i2 - N_GROUPS
    oh1 = lane == eid1
    oh2 = lane == eid2
    cnt = oh1.astype(F32) + oh2.astype(F32)
    r_i = lax.broadcasted_iota(jnp.int32, (tm, tm), 0)
    c_i = lax.broadcasted_iota(jnp.int32, (tm, tm), 1)
    tri = jnp.where(c_i < r_i, 1.0, 0.0).astype(BF16)
    before = _dot(tri, cnt.astype(BF16)) + carry[...]
    rank1 = jnp.sum(jnp.where(oh1, before, 0.0), axis=1, keepdims=True)
    rank2 = jnp.sum(jnp.where(oh2, before, 0.0), axis=1, keepdims=True)
    carry[...] = carry[...] + jnp.sum(cnt, axis=0, keepdims=True)
    count_ref[...] = carry[...]

    rec = jnp.zeros(logits.shape, F32)
    for off, val in ((ROUTE_EID, eid1.astype(F32)), (ROUTE_EID + 1, eid2.astype(F32)),
                     (ROUTE_RANK, rank1), (ROUTE_RANK + 1, rank2),
                     (ROUTE_COMB, g_prob * p1), (ROUTE_COMB + 1, g_prob * p2)):
        rec = jnp.where(lane == off, val, rec)
    route_ref[...] = rec


def _route(logits):
    m = logits.shape[0]
    tm = _pick(m, (256, 128))
    return pl.pallas_call(
        _route_kernel,
        grid=(m // tm,),
        in_specs=[pl.BlockSpec((tm, LANES), lambda i: (i, 0))],
        out_specs=[pl.BlockSpec((tm, LANES), lambda i: (i, 0)), pl.BlockSpec((1, LANES), lambda i: (0, 0))],
        out_shape=[jax.ShapeDtypeStruct((m, LANES), F32), jax.ShapeDtypeStruct((1, LANES), F32)],
        scratch_shapes=[pltpu.VMEM((1, LANES), F32)],
        compiler_params=_params("arbitrary"),
        name="route",
    )(logits)


def _dispatch_kernel(pos_ref, free_ref, src_ref, xs_ref, zrow, sems, *, tm, n_ranges):
    i = pl.program_id(0)
    base = i * (tm * TOP_K)

    @pl.when(i == 0)
    def _():
        zrow[...] = jnp.zeros_like(zrow)

        def zero_copy(p):
            return pltpu.make_async_copy(zrow.at[pl.ds(0, 1)], xs_ref.at[pl.ds(p, 1)], sems.at[1])

        def over_free_rows(fn):
            def one_range(r, carry):
                return lax.fori_loop(free_ref[2 * r], free_ref[2 * r + 1], fn, carry)
            lax.fori_loop(0, n_ranges, one_range, 0)

        def start(p, carry):
            zero_copy(p).start()
            return carry

        def wait(p, carry):
            zero_copy(p).wait()
            return carry

        over_free_rows(start)
        over_free_rows(wait)

    def row_copy(r, p):
        return pltpu.make_async_copy(src_ref.at[pl.ds(r, 1)], xs_ref.at[pl.ds(p, 1)], sems.at[0])

    def issue(r, carry):
        for k in range(TOP_K):
            row_copy(r, pos_ref[base + TOP_K * r + k]).start()
        return carry

    def drain(r, carry):
        for k in range(TOP_K):
            row_copy(r, pos_ref[base + TOP_K * r + k]).wait()
        return carry

    lax.fori_loop(0, tm, issue, 0, unroll=4)
    lax.fori_loop(0, tm, drain, 0)


def _dispatch(pos, free_rows, src, n_rows):
    m, d = src.shape
    tm = _pick(m, (256, 128))
    n_ranges = free_rows.shape[0] // 2
    return pl.pallas_call(
        functools.partial(_dispatch_kernel, tm=tm, n_ranges=n_ranges),
        grid_spec=pltpu.PrefetchScalarGridSpec(
            num_scalar_prefetch=2,
            grid=(m // tm,),
            in_specs=[pl.BlockSpec((tm, d), lambda i, pos, free: (i, 0))],
            out_specs=pl.BlockSpec(memory_space=pl.ANY),
            scratch_shapes=[pltpu.VMEM((SUBLANES, d), src.dtype), pltpu.SemaphoreType.DMA((2,))]),
        out_shape=jax.ShapeDtypeStruct((n_rows, d), src.dtype),
        compiler_params=_params("arbitrary"),
        name="dispatch",
    )(pos, free_rows, src)


STEP_VALID, STEP_NEW_WEIGHTS, STEP_HAS_NEXT = 1, 2, 4
(UP_B, UP_E, UP_F, UP_NEXT_E, UP_NEXT_F, UP_OUT_B, UP_OUT_F, UP_FLAGS, UP_FIELDS) = range(9)
(DN_HDN_B, DN_E, DN_NEXT_E, DN_FLAGS, DN_FIELDS) = range(5)


def _expert_up_kernel(tab, xs_ref, wg_hbm, wu_hbm, hdn_ref, stage_g, stage_u, wg_bf, wu_bf, sems, *, tf):
    s = pl.program_id(0)
    field = lambda k: tab[s * UP_FIELDS + k]
    flags = field(UP_FLAGS)

    def weight_copies(e, f):
        cols = pl.ds(pl.multiple_of(f * tf, tf), tf)
        return (pltpu.make_async_copy(wg_hbm.at[e, :, cols], stage_g, sems.at[0]),
                pltpu.make_async_copy(wu_hbm.at[e, :, cols], stage_u, sems.at[1]))

    @pl.when(s == 0)
    def _():
        for c in weight_copies(field(UP_E), field(UP_F)):
            c.start()

    @pl.when((flags & STEP_NEW_WEIGHTS) != 0)
    def _():
        for c in weight_copies(field(UP_E), field(UP_F)):
            c.wait()
        wg_bf[...] = stage_g[...].astype(BF16)
        wu_bf[...] = stage_u[...].astype(BF16)

        @pl.when((flags & STEP_HAS_NEXT) != 0)
        def _():
            for c in weight_copies(field(UP_NEXT_E), field(UP_NEXT_F)):
                c.start()

    @pl.when((flags & STEP_VALID) != 0)
    def _():
        x = xs_ref[...].astype(BF16)
        hg = _dot(x, wg_bf[...])
        hu = _dot(x, wu_bf[...])
        hdn_ref[...] = (jax.nn.silu(hg) * hu).astype(hdn_ref.dtype)

    @pl.when((flags & STEP_VALID) == 0)
    def _():
        hdn_ref[...] = jnp.zeros_like(hdn_ref)


def _expert_down_kernel(tab, hdn_ref, wd_hbm, y_ref, stage, wd_bf, sem):
    b = pl.program_id(0)
    field = lambda k: tab[b * DN_FIELDS + k]
    flags = field(DN_FLAGS)

    def weight_copy(e):
        return pltpu.make_async_copy(wd_hbm.at[e], stage, sem)

    @pl.when(b == 0)
    def _():
        weight_copy(field(DN_E)).start()

    @pl.when((flags & STEP_NEW_WEIGHTS) != 0)
    def _():
        weight_copy(field(DN_E)).wait()
        wd_bf[...] = stage[...].astype(BF16)

        @pl.when((flags & STEP_HAS_NEXT) != 0)
        def _():
            weight_copy(field(DN_NEXT_E)).start()

    @pl.when((flags & STEP_VALID) != 0)
    def _():
        y_ref[...] = _dot(hdn_ref[...], wd_bf[...])

    @pl.when((flags & STEP_VALID) == 0)
    def _():
        y_ref[...] = jnp.zeros_like(y_ref)


def _experts(plan, xs, w_gate, w_up, w_down, tg, tf):
    n_rows, d = xs.shape
    _, _, d_exp = w_gate.shape
    nb = n_rows // tg
    n_steps = nb * (d_exp // tf)
    assert plan['up'].shape[0] == n_steps * UP_FIELDS and plan['down'].shape[0] == nb * DN_FIELDS

    hdn = pl.pallas_call(
        functools.partial(_expert_up_kernel, tf=tf),
        grid_spec=pltpu.PrefetchScalarGridSpec(
            num_scalar_prefetch=1,
            grid=(n_steps,),
            in_specs=[pl.BlockSpec((tg, d), lambda s, t: (t[s * UP_FIELDS + UP_B], 0)),
                      pl.BlockSpec(memory_space=pl.ANY),
                      pl.BlockSpec(memory_space=pl.ANY)],
            out_specs=pl.BlockSpec((tg, tf), lambda s, t: (t[s * UP_FIELDS + UP_OUT_B], t[s * UP_FIELDS + UP_OUT_F])),
            scratch_shapes=[pltpu.VMEM((d, tf), F32), pltpu.VMEM((d, tf), F32),
                            pltpu.VMEM((d, tf), BF16), pltpu.VMEM((d, tf), BF16),
                            pltpu.SemaphoreType.DMA((2,))]),
        out_shape=jax.ShapeDtypeStruct((n_rows, d_exp), BF16),
        compiler_params=_params("arbitrary"),
        name="expert_up",
    )(plan['up'], xs, w_gate, w_up)

    return pl.pallas_call(
        _expert_down_kernel,
        grid_spec=pltpu.PrefetchScalarGridSpec(
            num_scalar_prefetch=1,
            grid=(nb,),
            in_specs=[pl.BlockSpec((tg, d_exp), lambda b, t: (t[b * DN_FIELDS + DN_HDN_B], 0)),
                      pl.BlockSpec(memory_space=pl.ANY)],
            out_specs=pl.BlockSpec((tg, d), lambda b, t: (b, 0)),
            scratch_shapes=[pltpu.VMEM((d_exp, d), F32), pltpu.VMEM((d_exp, d), BF16),
                            pltpu.SemaphoreType.DMA(())]),
        out_shape=jax.ShapeDtypeStruct((n_rows, d), F32),
        compiler_params=_params("arbitrary"),
        name="expert_down",
    )(plan['down'], hdn, w_down)


def _ple_kernel(h1b_ref, wg_ref, p_ref, wp_ref, h1_ref, o_ref, *, alpha):
    gate = jax.nn.sigmoid(_dot(h1b_ref[...], wg_ref[...]))
    o_ref[...] = alpha * h1_ref[...] + gate * _dot(p_ref[...], wp_ref[...])


def _ple(h1b, w_gate, p, w_proj, h1, alpha):
    m, d = h1b.shape
    n = w_gate.shape[1]
    dp = p.shape[1]
    tm = _pick(m, (1024, 512, 256, 128))
    tn = _pick(n, (512, 256, 128))
    return pl.pallas_call(
        functools.partial(_ple_kernel, alpha=alpha),
        grid=(m // tm, n // tn),
        in_specs=[pl.BlockSpec((tm, d), lambda i, j: (i, 0)),
                  pl.BlockSpec((d, tn), lambda i, j: (0, j)),
                  pl.BlockSpec((tm, dp), lambda i, j: (i, 0)),
                  pl.BlockSpec((dp, tn), lambda i, j: (0, j)),
                  pl.BlockSpec((tm, tn), lambda i, j: (i, j))],
        out_specs=pl.BlockSpec((tm, tn), lambda i, j: (i, j)),
        out_shape=jax.ShapeDtypeStruct((m, n), F32),
        compiler_params=_params("parallel", "arbitrary"),
        name="ple",
    )(h1b, w_gate, p, w_proj, h1)


def _combine_kernel(pos_ref, pre_ref, route_ref, g_ref, b_ref, y_hbm, op_ref, os_ref, ybuf, sems, *, tm, npb):
    i = pl.program_id(0)
    n_blocks = pl.num_programs(0)
    slot = i % 2

    def row_copy(blk, buf, r, k):
        p = pos_ref[blk * (tm * TOP_K) + TOP_K * r + k]
        return pltpu.make_async_copy(y_hbm.at[pl.ds(p, 1)], ybuf.at[buf, k, pl.ds(r, 1)], sems.at[buf])

    def gather(blk, buf):
        def issue(r, carry):
            for k in range(TOP_K):
                row_copy(blk, buf, r, k).start()
            return carry
        lax.fori_loop(0, tm, issue, 0, unroll=4)

    @pl.when(i == 0)
    def _():
        gather(0, 0)

    @pl.when(i + 1 < n_blocks)
    def _():
        gather(i + 1, 1 - slot)

    def drain(r, carry):
        for k in range(TOP_K):
            row_copy(i, slot, r, k).wait()
        return carry

    lax.fori_loop(0, tm, drain, 0)

    acc = pre_ref[...]
    for k in range(TOP_K):
        acc = acc + ybuf[slot, k] * route_ref[:, ROUTE_COMB + k:ROUTE_COMB + k + 1]
    res = _layernorm_rows(acc, g_ref[...], b_ref[...])

    @pl.when(i < npb)
    def _():
        op_ref[...] = res

    @pl.when(i >= npb)
    def _():
        os_ref[...] = res


def _combine(pos, pre, route, g, b, y, rows):
    m, d = pre.shape
    tm = _pick(np.gcd(rows[0], rows[1]), (256, 128))
    npb = rows[0] // tm
    out_p, out_s = _group_specs((tm, d), npb, lambda pos: 0)
    return pl.pallas_call(
        functools.partial(_combine_kernel, tm=tm, npb=npb),
        grid_spec=pltpu.PrefetchScalarGridSpec(
            num_scalar_prefetch=1,
            grid=(m // tm,),
            in_specs=[pl.BlockSpec((tm, d), lambda i, pos: (i, 0)),
                      pl.BlockSpec((tm, LANES), lambda i, pos: (i, 0)),
                      pl.BlockSpec((1, d), lambda i, pos: (0, 0)),
                      pl.BlockSpec((1, d), lambda i, pos: (0, 0)),
                      pl.BlockSpec(memory_space=pl.ANY)],
            out_specs=[out_p, out_s],
            scratch_shapes=[pltpu.VMEM((2, TOP_K, tm, d), F32), pltpu.SemaphoreType.DMA((2,))]),
        out_shape=[jax.ShapeDtypeStruct((rows[0], d), F32), jax.ShapeDtypeStruct((rows[1], d), F32)],
        compiler_params=_params("arbitrary"),
        name="combine_ln2",
    )(pos, pre, route, g, b, y)


def _next_flagged(flag, values):
    n = flag.shape[0]
    idx = jnp.arange(n, dtype=jnp.int32)
    at_or_after = lax.cummin(jnp.where(flag, idx, n)[::-1])[::-1]
    nxt = jnp.concatenate([at_or_after[1:], jnp.full((1,), n, jnp.int32)])
    return [v[jnp.minimum(nxt, n - 1)] for v in values], nxt < n


def _moe_plan(route, counts, n_tok, tg, nf):
    i32 = jnp.int32
    eid = route[:, ROUTE_EID:ROUTE_EID + TOP_K].astype(i32)
    rank = route[:, ROUTE_RANK:ROUTE_RANK + TOP_K].astype(i32)
    cnt = counts[0, :N_EXPERTS].astype(i32)
    nblk_e = (cnt + tg - 1) // tg
    end_e = jnp.cumsum(nblk_e)
    start_e = end_e - nblk_e
    pos = (start_e[eid] * tg + rank).reshape(-1)
    nb = -(-(n_tok * TOP_K) // tg) + N_EXPERTS
    n_used = end_e[-1]
    blk = jnp.arange(nb, dtype=i32)
    e_of_blk = jnp.sum(end_e[None, :] <= blk[:, None], axis=1).astype(i32)

    run_start = jnp.concatenate([start_e, n_used[None]])
    run_len = jnp.concatenate([nblk_e, (nb - n_used)[None]])
    s = jnp.arange(nb * nf, dtype=i32)
    e_s = e_of_blk[s // nf]
    r0 = run_start[e_s]
    n = jnp.maximum(run_len[e_s], 1)
    local = s - r0 * nf
    f_s = local // n
    b_s = r0 + local % n
    valid = e_s < N_EXPERTS
    last = n_used * nf - 1
    hold = lambda t: jnp.where(valid, t, t[last])
    ob_s, of_s = b_s, f_s
    b_s, f_s, e_s = hold(b_s), hold(f_s), hold(e_s)
    prev = jnp.maximum(s - 1, 0)
    new_w = valid & ((s == 0) | (e_s != e_s[prev]) | (f_s != f_s[prev]))
    (ne_s, nf_s), has_next = _next_flagged(new_w, (e_s, f_s))
    flags = (valid * STEP_VALID + new_w * STEP_NEW_WEIGHTS + (new_w & has_next) * STEP_HAS_NEXT).astype(i32)
    up = jnp.stack([b_s, e_s, f_s, ne_s, nf_s, ob_s, of_s, flags], axis=1).reshape(-1).astype(i32)

    used = blk < n_used
    hb = jnp.minimum(blk, n_used - 1)
    e_b = e_of_blk[hb]
    new_e = used & ((blk == 0) | (e_b != e_b[jnp.maximum(blk - 1, 0)]))
    (ne_b,), has_next_b = _next_flagged(new_e, (e_b,))
    dflags = (used * STEP_VALID + new_e * STEP_NEW_WEIGHTS + (new_e & has_next_b) * STEP_HAS_NEXT).astype(i32)
    down = jnp.stack([hb, e_b, ne_b, dflags], axis=1).reshape(-1).astype(i32)
    free_lo = jnp.concatenate([start_e * tg + cnt, (n_used * tg)[None]])
    free_hi = jnp.concatenate([end_e * tg, jnp.full((1,), nb * tg, i32)])
    free_rows = jnp.stack([free_lo, free_hi], axis=1).reshape(-1).astype(i32)
    return pos, free_rows, dict(up=up, down=down), nb


def _layer(xs, ps, states, w, alpha):
    d_model = xs[0].shape[-1]
    d_rnn = w['w_conv'].shape[1]
    dk, dv = states[0][2].shape[-2:]
    ret_qk = RET_HEADS * dk
    shapes = [x.shape[:2] for x in xs]
    n_rows = [b * l for b, l in shapes]
    row0 = [0, n_rows[0]]
    n_tok = sum(n_rows)

    x2d = [x.reshape(-1, d_model) for x in xs]
    p_all = jnp.concatenate([p.reshape(-1, p.shape[-1]) for p in ps], axis=0).astype(BF16)
    z = _inproj(x2d, w['w_in'])

    q_col = 2 * d_rnn // ret_qk
    v_col = (2 * d_rnn + 2 * ret_qk) // (RET_HEADS * dv)
    ret_cols = (q_col, q_col + 1, v_col, v_col + 1)
    assert 2 * d_rnn % ret_qk == 0 and (2 * d_rnn + 2 * ret_qk) % (RET_HEADS * dv) == 0

    ya, yb, new_states = [], [], []
    for gi, ((bsz, seq_len), (conv0, h0, s0)) in enumerate(zip(shapes, states)):
        y_a, conv_n, h_n = _rglru(z, conv0, h0.reshape(bsz, 1, d_rnn), row0[gi], bsz, seq_len,
                                  w['w_conv'], w['b_conv'], w['w_rgate'], w['b_rgate'], w['w_igate'],
                                  w['b_igate'], w['lru_lambda'])
        start = 0 if gi == 0 else PAST_LEN
        cosf, sinf = _rope_tables(start + jnp.arange(seq_len, dtype=jnp.int32), dk)
        y_b, s_n = _retention(z, s0, cosf, sinf, row0[gi], bsz, seq_len, ret_cols)
        ya.append(y_a)
        yb.append(y_b)
        new_states.append((conv_n, h_n.reshape(bsz, d_rnn), s_n))

    pre1 = _outproj(ya, yb, w['w_out'], x2d, alpha)
    h1, h1b, logits = _ln_router(pre1, w['ln1_g'], w['ln1_b'], w['w_router'])
    route, counts = _route(logits)

    tg = 256
    d_exp = w['w_gate'].shape[2]
    tf = _pick(d_exp, (512, 256, 128))
    pos, free_rows, plan, n_blocks = _moe_plan(route, counts, n_tok, tg, d_exp // tf)
    xs_sorted = _dispatch(pos, free_rows, h1, n_blocks * tg)
    y_sorted = _experts(plan, xs_sorted, w['w_gate'], w['w_up'], w['w_down'], tg, tf)
    pre2 = _ple(h1b, w['w_ple_gate'], p_all, w['w_ple_proj'], h1, alpha)
    h2 = _combine(pos, pre2, route, w['ln2_g'], w['ln2_b'], y_sorted, n_rows)

    outs = [h.reshape(b, l, d_model) for h, (b, l) in zip(h2, shapes)]
    return outs, new_states


def kernel(x_prompt, x_sample, state_rglru_conv, state_rglru_h, state_retention, p_prompt, p_sample, w_in, w_conv, b_conv, w_rgate, b_rgate, w_igate, b_igate, lru_lambda, w_out, ln1_g, ln1_b, w_router_group, w_router_expert, w_gate, w_up, w_down, w_ple_gate, w_ple_proj, ln2_g, ln2_b):
    depth = w_in.shape[0]
    alpha = (2.0 * depth) ** 0.25
    bp = x_prompt.shape[0]
    d_model = x_prompt.shape[-1]
    d_rnn = w_conv.shape[-1]
    xs = [x_prompt, x_sample]
    new = [[], []]
    for i in range(depth):
        router = jnp.concatenate([w_router_group[i], w_router_expert[i].reshape(d_model, N_EXPERTS)], axis=1)
        router = jnp.pad(router, ((0, 0), (0, LANES - router.shape[1])))
        w = dict(w_in=w_in[i].astype(BF16), w_conv=w_conv[i], b_conv=b_conv[i].reshape(1, -1),
                 w_rgate=w_rgate[i].astype(BF16), b_rgate=b_rgate[i].reshape(1, -1),
                 w_igate=w_igate[i].astype(BF16), b_igate=b_igate[i].reshape(1, -1),
                 lru_lambda=lru_lambda[i].reshape(1, -1), w_out=w_out[i].astype(BF16),
                 ln1_g=ln1_g[i].reshape(1, -1), ln1_b=ln1_b[i].reshape(1, -1), w_router=router.astype(BF16),
                 w_gate=w_gate[i], w_up=w_up[i], w_down=w_down[i], w_ple_gate=w_ple_gate[i].astype(BF16),
                 w_ple_proj=w_ple_proj[i].astype(BF16), ln2_g=ln2_g[i].reshape(1, -1),
                 ln2_b=ln2_b[i].reshape(1, -1))
        zero_states = (jnp.zeros((bp, CONV_W - 1, d_rnn), x_prompt.dtype),
                       jnp.zeros((bp, d_rnn), x_prompt.dtype),
                       jnp.zeros((bp,) + state_retention.shape[2:], x_prompt.dtype))
        states = [zero_states, (state_rglru_conv[i], state_rglru_h[i], state_retention[i])]
        xs, st = _layer(xs, [p_prompt[i], p_sample[i]], states, w, alpha)
        for gi in range(2):
            new[gi].append(st[gi])
    stack = lambda gi, k: jnp.stack([s[k] for s in new[gi]])
    return (xs[0], xs[1], stack(0, 0), stack(0, 1), stack(0, 2), stack(1, 0), stack(1, 1), stack(1, 2))
```

```python
import functools

import numpy as np
import jax
import jax.numpy as jnp
from jax import lax
from jax.experimental import pallas as pl
from jax.experimental.pallas import tpu as pltpu

CHUNK = 64
RNN_BLOCKS = 16
CONV_W = 4
LRU_C = 8.0
RET_HEADS = 8
ROPE_BASE = 10000.0
N_GROUPS = 4
EXPERTS_PER_GROUP = 8
N_EXPERTS = N_GROUPS * EXPERTS_PER_GROUP
TOP_K = 2
PAST_LEN = 4096
LN_EPS = 1e-5

LANES = 128
SUBLANES = 8
V7X_VMEM_BYTES = 64 * 1024 * 1024
VMEM_LIMIT = V7X_VMEM_BYTES * 7 // 8

F32 = jnp.float32
BF16 = jnp.bfloat16


def _pick(n, cands):
    for c in cands:
        if n % c == 0:
            return c
    raise ValueError(f"no tile in {cands} divides {n}")


def _params(*sem):
    return pltpu.CompilerParams(dimension_semantics=sem, vmem_limit_bytes=VMEM_LIMIT)


def _dot(a, b):
    return jnp.dot(a, b, preferred_element_type=F32)


def _group_specs(block, npb, col_of):
    prompt = pl.BlockSpec(block, lambda i, *a: (jnp.minimum(i, npb - 1), col_of(*a)))
    sample = pl.BlockSpec(block, lambda i, *a: (jnp.maximum(i - npb, 0), col_of(*a)))
    return prompt, sample


def _stack_cast_kernel(xp_ref, xs_ref, o_ref, *, npb):
    i = pl.program_id(0)

    @pl.when(i < npb)
    def _():
        o_ref[...] = xp_ref[...].astype(o_ref.dtype)

    @pl.when(i >= npb)
    def _():
        o_ref[...] = xs_ref[...].astype(o_ref.dtype)


def _stack_cast(x, dtype):
    rows = [a.shape[0] for a in x]
    k = x[0].shape[1]
    tm = _pick(np.gcd(rows[0], rows[1]), (512, 256, 128))
    npb = rows[0] // tm
    return pl.pallas_call(
        functools.partial(_stack_cast_kernel, npb=npb),
        grid=(sum(rows) // tm,),
        in_specs=[*_group_specs((tm, k), npb, lambda: 0)],
        out_specs=pl.BlockSpec((tm, k), lambda i: (i, 0)),
        out_shape=jax.ShapeDtypeStruct((sum(rows), k), dtype),
        compiler_params=_params("parallel"),
        name="stack_cast",
    )(x[0], x[1])


def _mm_kernel(a_ref, b_ref, o_ref):
    o_ref[...] = _dot(a_ref[...], b_ref[...])


def _matmul(a, b):
    m, k = a.shape
    n = b.shape[1]
    tm = _pick(m, (1024, 512, 256, 128))
    tn = _pick(n, (1024, 512, 256, 128))
    return pl.pallas_call(
        _mm_kernel,
        grid=(m // tm, n // tn),
        in_specs=[pl.BlockSpec((tm, k), lambda i, j: (i, 0)),
                  pl.BlockSpec((k, tn), lambda i, j: (0, j))],
        out_specs=pl.BlockSpec((tm, tn), lambda i, j: (i, j)),
        out_shape=jax.ShapeDtypeStruct((m, n), F32),
        compiler_params=_params("parallel", "arbitrary"),
        name="in_proj",
    )(a, b)


def _rglru_kernel(gate_ref, xr_ref, conv0_ref, h0_ref, wconv_ref, bconv_ref, wr_ref, br_ref, wi_ref, bi_ref,
                  lam_ref, y_ref, conv_out_ref, h_out_ref, xbuf, hc, a_buf, u_buf, h_buf, *, t_rows):
    j = pl.program_id(1)
    halo = CONV_W - 1
    top = 8

    @pl.when(j == 0)
    def _():
        xbuf[top - halo:top, :] = conv0_ref[0]
        hc[...] = h0_ref[0]

    xbuf[top:top + t_rows, :] = xr_ref[...]
    conv_out_ref[0] = xr_ref[t_rows - halo:t_rows, :]
    sp = jax.nn.softplus(-lam_ref[...])

    bw = wr_ref.shape[1]
    for n in range(RNN_BLOCKS):
        cs = slice(n * bw, (n + 1) * bw)
        xc = bconv_ref[:, cs]
        for w in range(CONV_W):
            xc = xc + xbuf[top - halo + w:top - halo + w + t_rows, cs] * wconv_ref[w:w + 1, cs]
        xb = xc.astype(BF16)
        r = jax.nn.sigmoid(_dot(xb, wr_ref[n]) + br_ref[:, cs])
        ig = jax.nn.sigmoid(_dot(xb, wi_ref[n]) + bi_ref[:, cs])
        log_a = -LRU_C * r * sp[:, cs]
        a = jnp.exp(log_a)
        a_buf[:, cs] = a
        u_buf[:, cs] = jnp.sqrt(1.0 - a * a) * (ig * xc)

    xbuf[top - halo:top, :] = xr_ref[t_rows - halo:t_rows, :]

    def step(t, h):
        h = a_buf[pl.ds(t, 1), :] * h + u_buf[pl.ds(t, 1), :]
        h_buf[pl.ds(t, 1), :] = h
        return h

    h_last = lax.fori_loop(0, t_rows, step, hc[...], unroll=8)
    hc[...] = h_last
    h_out_ref[0] = h_last
    y_ref[...] = (jax.nn.gelu(gate_ref[...]) * h_buf[...]).astype(y_ref.dtype)


def _rglru(z, conv0, h0, row0, n_seq, seq_len, wconv, bconv, wr, br, wi, bi, lam):
    d_rnn = wconv.shape[1]
    t_rows = _pick(seq_len, (256, 128, 64))
    nb = seq_len // t_rows
    rb0 = row0 // t_rows
    assert row0 % t_rows == 0

    def rows(col):
        return lambda s, j: (rb0 + s * nb + j, col)

    full = lambda shape: pl.BlockSpec(shape, lambda s, j: (0,) * len(shape))
    kern = functools.partial(_rglru_kernel, t_rows=t_rows)
    return pl.pallas_call(
        kern,
        grid=(n_seq, nb),
        in_specs=[pl.BlockSpec((t_rows, d_rnn), rows(0)),
                  pl.BlockSpec((t_rows, d_rnn), rows(1)),
                  pl.BlockSpec((1, CONV_W - 1, d_rnn), lambda s, j: (s, 0, 0)),
                  pl.BlockSpec((1, 1, d_rnn), lambda s, j: (s, 0, 0)),
                  full(wconv.shape), full(bconv.shape), full(wr.shape), full(br.shape),
                  full(wi.shape), full(bi.shape), full(lam.shape)],
        out_specs=[pl.BlockSpec((t_rows, d_rnn), lambda s, j: (s * nb + j, 0)),
                   pl.BlockSpec((1, CONV_W - 1, d_rnn), lambda s, j: (s, 0, 0)),
                   pl.BlockSpec((1, 1, d_rnn), lambda s, j: (s, 0, 0))],
        out_shape=[jax.ShapeDtypeStruct((n_seq * seq_len, d_rnn), BF16),
                   jax.ShapeDtypeStruct((n_seq, CONV_W - 1, d_rnn), F32),
                   jax.ShapeDtypeStruct((n_seq, 1, d_rnn), F32)],
        scratch_shapes=[pltpu.VMEM((t_rows + 8, d_rnn), F32),
                        pltpu.VMEM((1, d_rnn), F32),
                        pltpu.VMEM((t_rows, d_rnn), F32),
                        pltpu.VMEM((t_rows, d_rnn), F32),
                        pltpu.VMEM((t_rows, d_rnn), F32)],
        compiler_params=_params("arbitrary", "arbitrary"),
        name="rglru",
    )(z, z, conv0, h0, wconv, bconv, wr, br, wi, bi, lam)


RET_CHUNKS_PER_STEP = 8


def _retention_consts(dk):
    log_gamma = np.log1p(-np.exp2(-5.0 - np.arange(RET_HEADS, dtype=np.float32))).astype(np.float32)
    idx = np.arange(CHUNK, dtype=np.float32)
    dmat = np.exp(log_gamma[:, None, None] * np.abs(idx[:, None] - idx[None, :])).astype(np.float32)
    q_dec = np.exp(log_gamma[:, None] * (idx[None, :] + 1.0)).astype(np.float32)
    k_dec = np.exp(log_gamma[:, None] * (CHUNK - 1.0 - idx[None, :])).astype(np.float32)
    s_dec = np.exp(log_gamma * CHUNK).astype(np.float32)
    q_dec = np.broadcast_to(q_dec[:, :, None], (RET_HEADS, CHUNK, LANES))
    k_dec = np.broadcast_to(k_dec[:, :, None], (RET_HEADS, CHUNK, dk))
    return jnp.asarray(dmat), jnp.asarray(q_dec), jnp.asarray(k_dec), jnp.asarray(s_dec)


def _retention_kernel(sdec_ref, q_ref, k_ref, v_ref, g_ref, cos_ref, sin_ref, dmat_ref, qdec_ref, kdec_ref,
                      s0_ref, y_ref, s_out_ref, s_acc, *, dk, dv, chunks):
    c = pl.program_id(1)

    @pl.when(c == 0)
    def _():
        s_acc[...] = s0_ref[0]

    scale = dk ** -0.5

    for ci in range(chunks):
        rows = slice(ci * CHUNK, (ci + 1) * CHUNK)
        cosf = cos_ref[rows, :]
        sinf = sin_ref[rows, :]

        def rope(t):
            return t * cosf + pltpu.roll(t, dk // 2, axis=1) * sinf

        for h in range(RET_HEADS):
            qh = rope(q_ref[rows, h * dk:(h + 1) * dk]) * scale
            kh = rope(k_ref[rows, h * dk:(h + 1) * dk])
            vb = v_ref[rows, h * dv:(h + 1) * dv].astype(BF16)
            qb = qh.astype(BF16)
            kb = kh.astype(BF16)
            scores = lax.dot_general(qb, kb, (((1,), (1,)), ((), ())), preferred_element_type=F32) * dmat_ref[h]
            intra = _dot(scores.astype(BF16), vb)
            s_h = s_acc[h]
            qd = qdec_ref[h]
            cross = _dot(qb, s_h.astype(BF16)) * jnp.concatenate([qd] * (dv // LANES), axis=1)
            kd = (kh * kdec_ref[h]).astype(BF16)
            s_acc[h] = s_h * sdec_ref[h] + lax.dot_general(kd, vb, (((0,), (0,)), ((), ())),
                                                           preferred_element_type=F32)
            o = intra + cross
            mu = jnp.mean(o, axis=-1, keepdims=True)
            oc = o - mu
            var = jnp.mean(oc * oc, axis=-1, keepdims=True)
            o = oc * lax.rsqrt(var + LN_EPS)
            gh = g_ref[rows, h * dv:(h + 1) * dv]
            y_ref[rows, h * dv:(h + 1) * dv] = (jax.nn.silu(gh) * o).astype(y_ref.dtype)

    s_out_ref[0] = s_acc[...]


def _retention(z, s0, cosf, sinf, row0, n_seq, seq_len, cols):
    _, heads, dk, dv = s0.shape
    chunk = min(CHUNK, seq_len)
    assert chunk == CHUNK and heads == RET_HEADS
    chunks = _pick(seq_len // chunk, (RET_CHUNKS_PER_STEP, 1))
    t_rows = chunks * chunk
    nc = seq_len // t_rows
    rb0 = row0 // t_rows
    assert row0 % t_rows == 0
    dmat, q_dec, k_dec, s_dec = _retention_consts(dk)

    def rows(col):
        return lambda s, c: (rb0 + s * nc + c, col)

    full = lambda shape: pl.BlockSpec(shape, lambda s, c: (0,) * len(shape))
    kern = functools.partial(_retention_kernel, dk=dk, dv=dv, chunks=chunks)
    return pl.pallas_call(
        kern,
        grid=(n_seq, nc),
        in_specs=[pl.BlockSpec(memory_space=pltpu.SMEM),
                  pl.BlockSpec((t_rows, heads * dk), rows(cols[0])),
                  pl.BlockSpec((t_rows, heads * dk), rows(cols[1])),
                  pl.BlockSpec((t_rows, heads * dv), rows(cols[2])),
                  pl.BlockSpec((t_rows, heads * dv), rows(cols[3])),
                  pl.BlockSpec((t_rows, dk), lambda s, c: (c, 0)),
                  pl.BlockSpec((t_rows, dk), lambda s, c: (c, 0)),
                  full(dmat.shape), full(q_dec.shape), full(k_dec.shape),
                  pl.BlockSpec((1, heads, dk, dv), lambda s, c: (s, 0, 0, 0))],
        out_specs=[pl.BlockSpec((t_rows, heads * dv), lambda s, c: (s * nc + c, 0)),
                   pl.BlockSpec((1, heads, dk, dv), lambda s, c: (s, 0, 0, 0))],
        out_shape=[jax.ShapeDtypeStruct((n_seq * seq_len, heads * dv), BF16),
                   jax.ShapeDtypeStruct((n_seq, heads, dk, dv), F32)],
        scratch_shapes=[pltpu.VMEM((heads, dk, dv), F32)],
        compiler_params=_params("arbitrary", "arbitrary"),
        name="retention",
    )(s_dec, z, z, z, z, cosf, sinf, dmat, q_dec, k_dec, s0)


def _rope_tables(positions, dk):
    half = dk // 2
    inv = ROPE_BASE ** (-jnp.arange(half, dtype=F32) / half)
    ang = positions.astype(F32)[:, None] * inv[None, :]
    cos, sin = jnp.cos(ang), jnp.sin(ang)
    return jnp.concatenate([cos, cos], axis=1), jnp.concatenate([-sin, sin], axis=1)


def _outproj_kernel(yap_ref, yas_ref, ybp_ref, ybs_ref, w1_ref, w2_ref, xp_ref, xs_ref, o_ref, *, alpha, npb):
    i = pl.program_id(0)

    def emit(ya_ref, yb_ref, x_ref):
        mix = _dot(ya_ref[...], w1_ref[...]) + _dot(yb_ref[...], w2_ref[...])
        o_ref[...] = alpha * x_ref[...] + mix

    @pl.when(i < npb)
    def _():
        emit(yap_ref, ybp_ref, xp_ref)

    @pl.when(i >= npb)
    def _():
        emit(yas_ref, ybs_ref, xs_ref)


def _outproj(ya, yb, w_out, x, alpha):
    rows = [a.shape[0] for a in ya]
    ka, kb = ya[0].shape[1], yb[0].shape[1]
    n = w_out.shape[1]
    tm = _pick(np.gcd(rows[0], rows[1]), (1024, 512, 256, 128))
    tn = _pick(n, (512, 256, 128))
    npb = rows[0] // tm
    assert ka == kb
    zero = lambda j: 0
    col = lambda j: j
    return pl.pallas_call(
        functools.partial(_outproj_kernel, alpha=alpha, npb=npb),
        grid=(sum(rows) // tm, n // tn),
        in_specs=[*_group_specs((tm, ka), npb, zero), *_group_specs((tm, kb), npb, zero),
                  pl.BlockSpec((ka, tn), lambda i, j: (0, j)),
                  pl.BlockSpec((kb, tn), lambda i, j: (1, j)),
                  *_group_specs((tm, tn), npb, col)],
        out_specs=pl.BlockSpec((tm, tn), lambda i, j: (i, j)),
        out_shape=jax.ShapeDtypeStruct((sum(rows), n), F32),
        compiler_params=_params("parallel", "arbitrary"),
        name="out_proj",
    )(ya[0], ya[1], yb[0], yb[1], w_out, w_out, x[0], x[1])


def _layernorm_rows(x, g, b):
    mu = jnp.mean(x, axis=-1, keepdims=True)
    xc = x - mu
    var = jnp.mean(xc * xc, axis=-1, keepdims=True)
    return xc * lax.rsqrt(var + LN_EPS) * g + b


def _ln_router_kernel(pre_ref, g_ref, b_ref, wr_ref, h1_ref, h1b_ref, logit_ref):
    y = _layernorm_rows(pre_ref[...], g_ref[...], b_ref[...])
    h1_ref[...] = y
    yb = y.astype(BF16)
    h1b_ref[...] = yb
    logit_ref[...] = _dot(yb, wr_ref[...])


def _ln_router(pre, g, b, w_router):
    m, d = pre.shape
    tm = _pick(m, (256, 128))
    row = lambda i: (i, 0)
    fixed = lambda i: (0, 0)
    return pl.pallas_call(
        _ln_router_kernel,
        grid=(m // tm,),
        in_specs=[pl.BlockSpec((tm, d), row), pl.BlockSpec((1, d), fixed), pl.BlockSpec((1, d), fixed),
                  pl.BlockSpec((d, LANES), fixed)],
        out_specs=[pl.BlockSpec((tm, d), row), pl.BlockSpec((tm, d), row), pl.BlockSpec((tm, LANES), row)],
        out_shape=[jax.ShapeDtypeStruct((m, d), F32), jax.ShapeDtypeStruct((m, d), BF16),
                   jax.ShapeDtypeStruct((m, LANES), F32)],
        compiler_params=_params("parallel"),
        name="ln1_router",
    )(pre, g, b, w_router)


ROUTE_EID, ROUTE_RANK, ROUTE_COMB = 0, 2, 4


def _route_kernel(logit_ref, route_ref, count_ref, carry):
    i = pl.program_id(0)
    tm = logit_ref.shape[0]

    @pl.when(i == 0)
    def _():
        carry[...] = jnp.zeros_like(carry)

    logits = logit_ref[...]
    lane = lax.broadcasted_iota(jnp.int32, logits.shape, 1)
    neg = -jnp.inf
    big = jnp.int32(2 * LANES)

    def first_argmax(vals):
        top = jnp.max(vals, axis=1, keepdims=True)
        return top, jnp.min(jnp.where(vals == top, lane, big), axis=1, keepdims=True)

    gl = jnp.where(lane < N_GROUPS, logits, neg)
    g_max, g_sel = first_argmax(gl)
    g_prob = 1.0 / jnp.sum(jnp.where(lane < N_GROUPS, jnp.exp(logits - g_max), 0.0), axis=1, keepdims=True)
    lo = N_GROUPS + g_sel * EXPERTS_PER_GROUP
    el = jnp.where((lane >= lo) & (lane < lo + EXPERTS_PER_GROUP), logits, neg)
    v1, i1 = first_argmax(el)
    v2, i2 = first_argmax(jnp.where(lane == i1, neg, el))
    e2 = jnp.exp(v2 - v1)
    p1 = 1.0 / (1.0 + e2)
    p2 = e2 / (1.0 + e2)
    eid1 = i1 - N_GROUPS
    eid2 = i2 - N_GROUPS
    oh1 = lane == eid1
    oh2 = lane == eid2
    cnt = oh1.astype(F32) + oh2.astype(F32)
    r_i = lax.broadcasted_iota(jnp.int32, (tm, tm), 0)
    c_i = lax.broadcasted_iota(jnp.int32, (tm, tm), 1)
    tri = jnp.where(c_i < r_i, 1.0, 0.0).astype(BF16)
    before = _dot(tri, cnt.astype(BF16)) + carry[...]
    rank1 = jnp.sum(jnp.where(oh1, before, 0.0), axis=1, keepdims=True)
    rank2 = jnp.sum(jnp.where(oh2, before, 0.0), axis=1, keepdims=True)
    carry[...] = carry[...] + jnp.sum(cnt, axis=0, keepdims=True)
    count_ref[...] = carry[...]

    rec = jnp.zeros(logits.shape, F32)
    for off, val in ((ROUTE_EID, eid1.astype(F32)), (ROUTE_EID + 1, eid2.astype(F32)),
                     (ROUTE_RANK, rank1), (ROUTE_RANK + 1, rank2),
                     (ROUTE_COMB, g_prob * p1), (ROUTE_COMB + 1, g_prob * p2)):
        rec = jnp.where(lane == off, val, rec)
    route_ref[...] = rec


def _route(logits):
    m = logits.shape[0]
    tm = _pick(m, (256, 128))
    return pl.pallas_call(
        _route_kernel,
        grid=(m // tm,),
        in_specs=[pl.BlockSpec((tm, LANES), lambda i: (i, 0))],
        out_specs=[pl.BlockSpec((tm, LANES), lambda i: (i, 0)), pl.BlockSpec((1, LANES), lambda i: (0, 0))],
        out_shape=[jax.ShapeDtypeStruct((m, LANES), F32), jax.ShapeDtypeStruct((1, LANES), F32)],
        scratch_shapes=[pltpu.VMEM((1, LANES), F32)],
        compiler_params=_params("arbitrary"),
        name="route",
    )(logits)


def _dispatch_kernel(pos_ref, free_ref, src_ref, xs_ref, zrow, sems, *, tm, n_ranges):
    i = pl.program_id(0)

    @pl.when(i == 0)
    def _():
        zrow[...] = jnp.zeros_like(zrow)

        def zero_copy(p):
            return pltpu.make_async_copy(zrow.at[pl.ds(0, 1)], xs_ref.at[pl.ds(p, 1)], sems.at[1])

        def over_free_rows(fn):
            def one_range(r, carry):
                return lax.fori_loop(free_ref[2 * r], free_ref[2 * r + 1], fn, carry)
            lax.fori_loop(0, n_ranges, one_range, 0)

        def start(p, carry):
            zero_copy(p).start()
            return carry

        def wait(p, carry):
            zero_copy(p).wait()
            return carry

        over_free_rows(start)
        over_free_rows(wait)

    base = i * (tm * TOP_K)

    def row_copy(r, p):
        return pltpu.make_async_copy(src_ref.at[pl.ds(r, 1)], xs_ref.at[pl.ds(p, 1)], sems.at[0])

    def issue(r, carry):
        for k in range(TOP_K):
            row_copy(r, pos_ref[base + TOP_K * r + k]).start()
        return carry

    def drain(r, carry):
        for k in range(TOP_K):
            row_copy(r, pos_ref[base + TOP_K * r + k]).wait()
        return carry

    lax.fori_loop(0, tm, issue, 0, unroll=8)
    lax.fori_loop(0, tm, drain, 0)


def _dispatch(pos, free_rows, src, n_rows):
    m, d = src.shape
    tm = _pick(m, (256, 128))
    n_ranges = free_rows.shape[0] // 2
    return pl.pallas_call(
        functools.partial(_dispatch_kernel, tm=tm, n_ranges=n_ranges),
        grid_spec=pltpu.PrefetchScalarGridSpec(
            num_scalar_prefetch=2,
            grid=(m // tm,),
            in_specs=[pl.BlockSpec((tm, d), lambda i, pos, free: (i, 0))],
            out_specs=pl.BlockSpec(memory_space=pl.ANY),
            scratch_shapes=[pltpu.VMEM((SUBLANES, d), src.dtype), pltpu.SemaphoreType.DMA((2,))]),
        out_shape=jax.ShapeDtypeStruct((n_rows, d), src.dtype),
        compiler_params=_params("arbitrary"),
        name="dispatch",
    )(pos, free_rows, src)


STEP_VALID, STEP_NEW_WEIGHTS, STEP_HAS_NEXT = 1, 2, 4
(UP_B, UP_E, UP_F, UP_NEXT_E, UP_NEXT_F, UP_OUT_B, UP_OUT_F, UP_FLAGS, UP_FIELDS) = range(9)
(DN_HDN_B, DN_E, DN_NEXT_E, DN_FLAGS, DN_FIELDS) = range(5)


def _expert_up_kernel(tab, xs_ref, wg_hbm, wu_hbm, hdn_ref, stage_g, stage_u, wg_bf, wu_bf, sems, *, tf):
    s = pl.program_id(0)
    field = lambda k: tab[s * UP_FIELDS + k]
    flags = field(UP_FLAGS)

    def weight_copies(e, f):
        cols = pl.ds(pl.multiple_of(f * tf, tf), tf)
        return (pltpu.make_async_copy(wg_hbm.at[e, :, cols], stage_g, sems.at[0]),
                pltpu.make_async_copy(wu_hbm.at[e, :, cols], stage_u, sems.at[1]))

    @pl.when(s == 0)
    def _():
        for c in weight_copies(field(UP_E), field(UP_F)):
            c.start()

    @pl.when((flags & STEP_NEW_WEIGHTS) != 0)
    def _():
        for c in weight_copies(field(UP_E), field(UP_F)):
            c.wait()
        wg_bf[...] = stage_g[...].astype(BF16)
        wu_bf[...] = stage_u[...].astype(BF16)

        @pl.when((flags & STEP_HAS_NEXT) != 0)
        def _():
            for c in weight_copies(field(UP_NEXT_E), field(UP_NEXT_F)):
                c.start()

    @pl.when((flags & STEP_VALID) != 0)
    def _():
        x = xs_ref[...].astype(BF16)
        hg = _dot(x, wg_bf[...])
        hu = _dot(x, wu_bf[...])
        hdn_ref[...] = (jax.nn.silu(hg) * hu).astype(hdn_ref.dtype)

    @pl.when((flags & STEP_VALID) == 0)
    def _():
        hdn_ref[...] = jnp.zeros_like(hdn_ref)


def _expert_down_kernel(tab, hdn_ref, wd_hbm, y_ref, stage, wd_bf, sem):
    b = pl.program_id(0)
    field = lambda k: tab[b * DN_FIELDS + k]
    flags = field(DN_FLAGS)

    def weight_copy(e):
        return pltpu.make_async_copy(wd_hbm.at[e], stage, sem)

    @pl.when(b == 0)
    def _():
        weight_copy(field(DN_E)).start()

    @pl.when((flags & STEP_NEW_WEIGHTS) != 0)
    def _():
        weight_copy(field(DN_E)).wait()
        wd_bf[...] = stage[...].astype(BF16)

        @pl.when((flags & STEP_HAS_NEXT) != 0)
        def _():
            weight_copy(field(DN_NEXT_E)).start()

    @pl.when((flags & STEP_VALID) != 0)
    def _():
        y_ref[...] = _dot(hdn_ref[...], wd_bf[...])

    @pl.when((flags & STEP_VALID) == 0)
    def _():
        y_ref[...] = jnp.zeros_like(y_ref)


def _experts(plan, xs, w_gate, w_up, w_down, tg, tf):
    n_rows, d = xs.shape
    _, _, d_exp = w_gate.shape
    nb = n_rows // tg
    n_steps = nb * (d_exp // tf)
    assert plan['up'].shape[0] == n_steps * UP_FIELDS and plan['down'].shape[0] == nb * DN_FIELDS

    hdn = pl.pallas_call(
        functools.partial(_expert_up_kernel, tf=tf),
        grid_spec=pltpu.PrefetchScalarGridSpec(
            num_scalar_prefetch=1,
            grid=(n_steps,),
            in_specs=[pl.BlockSpec((tg, d), lambda s, t: (t[s * UP_FIELDS + UP_B], 0)),
                      pl.BlockSpec(memory_space=pl.ANY),
                      pl.BlockSpec(memory_space=pl.ANY)],
            out_specs=pl.BlockSpec((tg, tf), lambda s, t: (t[s * UP_FIELDS + UP_OUT_B], t[s * UP_FIELDS + UP_OUT_F])),
            scratch_shapes=[pltpu.VMEM((d, tf), F32), pltpu.VMEM((d, tf), F32),
                            pltpu.VMEM((d, tf), BF16), pltpu.VMEM((d, tf), BF16),
                            pltpu.SemaphoreType.DMA((2,))]),
        out_shape=jax.ShapeDtypeStruct((n_rows, d_exp), BF16),
        compiler_params=_params("arbitrary"),
        name="expert_up",
    )(plan['up'], xs, w_gate, w_up)

    return pl.pallas_call(
        _expert_down_kernel,
        grid_spec=pltpu.PrefetchScalarGridSpec(
            num_scalar_prefetch=1,
            grid=(nb,),
            in_specs=[pl.BlockSpec((tg, d_exp), lambda b, t: (t[b * DN_FIELDS + DN_HDN_B], 0)),
                      pl.BlockSpec(memory_space=pl.ANY)],
            out_specs=pl.BlockSpec((tg, d), lambda b, t: (b, 0)),
            scratch_shapes=[pltpu.VMEM((d_exp, d), F32), pltpu.VMEM((d_exp, d), BF16),
                            pltpu.SemaphoreType.DMA(())]),
        out_shape=jax.ShapeDtypeStruct((n_rows, d), F32),
        compiler_params=_params("arbitrary"),
        name="expert_down",
    )(plan['down'], hdn, w_down)


def _ple_kernel(h1b_ref, wg_ref, p_ref, wp_ref, h1_ref, o_ref, *, alpha):
    gate = jax.nn.sigmoid(_dot(h1b_ref[...], wg_ref[...]))
    o_ref[...] = alpha * h1_ref[...] + gate * _dot(p_ref[...], wp_ref[...])


def _ple(h1b, w_gate, p, w_proj, h1, alpha):
    m, d = h1b.shape
    n = w_gate.shape[1]
    dp = p.shape[1]
    tm = _pick(m, (1024, 512, 256, 128))
    tn = _pick(n, (512, 256, 128))
    return pl.pallas_call(
        functools.partial(_ple_kernel, alpha=alpha),
        grid=(m // tm, n // tn),
        in_specs=[pl.BlockSpec((tm, d), lambda i, j: (i, 0)),
                  pl.BlockSpec((d, tn), lambda i, j: (0, j)),
                  pl.BlockSpec((tm, dp), lambda i, j: (i, 0)),
                  pl.BlockSpec((dp, tn), lambda i, j: (0, j)),
                  pl.BlockSpec((tm, tn), lambda i, j: (i, j))],
        out_specs=pl.BlockSpec((tm, tn), lambda i, j: (i, j)),
        out_shape=jax.ShapeDtypeStruct((m, n), F32),
        compiler_params=_params("parallel", "arbitrary"),
        name="ple",
    )(h1b, w_gate, p, w_proj, h1)


def _combine_kernel(pos_ref, pre_ref, route_ref, g_ref, b_ref, y_hbm, op_ref, os_ref, ybuf, sems, *, tm, npb):
    i = pl.program_id(0)
    n_blocks = pl.num_programs(0)
    slot = i % 2

    def row_copy(blk, buf, r, k):
        p = pos_ref[blk * (tm * TOP_K) + TOP_K * r + k]
        return pltpu.make_async_copy(y_hbm.at[pl.ds(p, 1)], ybuf.at[buf, k, pl.ds(r, 1)], sems.at[buf])

    def gather(blk, buf):
        def issue(r, carry):
            for k in range(TOP_K):
                row_copy(blk, buf, r, k).start()
            return carry
        lax.fori_loop(0, tm, issue, 0, unroll=8)

    @pl.when(i == 0)
    def _():
        gather(0, 0)

    @pl.when(i + 1 < n_blocks)
    def _():
        gather(i + 1, 1 - slot)

    def drain(r, carry):
        for k in range(TOP_K):
            row_copy(i, slot, r, k).wait()
        return carry

    lax.fori_loop(0, tm, drain, 0)

    acc = pre_ref[...]
    for k in range(TOP_K):
        acc = acc + ybuf[slot, k] * route_ref[:, ROUTE_COMB + k:ROUTE_COMB + k + 1]
    res = _layernorm_rows(acc, g_ref[...], b_ref[...])

    @pl.when(i < npb)
    def _():
        op_ref[...] = res

    @pl.when(i >= npb)
    def _():
        os_ref[...] = res


def _combine(pos, pre, route, g, b, y, rows):
    m, d = pre.shape
    tm = _pick(np.gcd(rows[0], rows[1]), (256, 128))
    npb = rows[0] // tm
    out_p, out_s = _group_specs((tm, d), npb, lambda pos: 0)
    return pl.pallas_call(
        functools.partial(_combine_kernel, tm=tm, npb=npb),
        grid_spec=pltpu.PrefetchScalarGridSpec(
            num_scalar_prefetch=1,
            grid=(m // tm,),
            in_specs=[pl.BlockSpec((tm, d), lambda i, pos: (i, 0)),
                      pl.BlockSpec((tm, LANES), lambda i, pos: (i, 0)),
                      pl.BlockSpec((1, d), lambda i, pos: (0, 0)),
                      pl.BlockSpec((1, d), lambda i, pos: (0, 0)),
                      pl.BlockSpec(memory_space=pl.ANY)],
            out_specs=[out_p, out_s],
            scratch_shapes=[pltpu.VMEM((2, TOP_K, tm, d), F32), pltpu.SemaphoreType.DMA((2,))]),
        out_shape=[jax.ShapeDtypeStruct((rows[0], d), F32), jax.ShapeDtypeStruct((rows[1], d), F32)],
        compiler_params=_params("arbitrary"),
        name="combine_ln2",
    )(pos, pre, route, g, b, y)


def _next_flagged(flag, values):
    n = flag.shape[0]
    idx = jnp.arange(n, dtype=jnp.int32)
    at_or_after = lax.cummin(jnp.where(flag, idx, n)[::-1])[::-1]
    nxt = jnp.concatenate([at_or_after[1:], jnp.full((1,), n, jnp.int32)])
    return [v[jnp.minimum(nxt, n - 1)] for v in values], nxt < n


def _moe_plan(route, counts, n_tok, tg, nf):
    i32 = jnp.int32
    eid = route[:, ROUTE_EID:ROUTE_EID + TOP_K].astype(i32)
    rank = route[:, ROUTE_RANK:ROUTE_RANK + TOP_K].astype(i32)
    cnt = counts[0, :N_EXPERTS].astype(i32)
    nblk_e = (cnt + tg - 1) // tg
    end_e = jnp.cumsum(nblk_e)
    start_e = end_e - nblk_e
    pos = (start_e[eid] * tg + rank).reshape(-1)
    nb = -(-(n_tok * TOP_K) // tg) + N_EXPERTS
    n_used = end_e[-1]
    blk = jnp.arange(nb, dtype=i32)
    e_of_blk = jnp.sum(end_e[None, :] <= blk[:, None], axis=1).astype(i32)

    run_start = jnp.concatenate([start_e, n_used[None]])
    run_len = jnp.concatenate([nblk_e, (nb - n_used)[None]])
    s = jnp.arange(nb * nf, dtype=i32)
    e_s = e_of_blk[s // nf]
    r0 = run_start[e_s]
    n = jnp.maximum(run_len[e_s], 1)
    local = s - r0 * nf
    f_s = local // n
    b_s = r0 + local % n
    valid = e_s < N_EXPERTS
    last = n_used * nf - 1
    hold = lambda t: jnp.where(valid, t, t[last])
    ob_s, of_s = b_s, f_s
    b_s, f_s, e_s = hold(b_s), hold(f_s), hold(e_s)
    prev = jnp.maximum(s - 1, 0)
    new_w = valid & ((s == 0) | (e_s != e_s[prev]) | (f_s != f_s[prev]))
    (ne_s, nf_s), has_next = _next_flagged(new_w, (e_s, f_s))
    flags = (valid * STEP_VALID + new_w * STEP_NEW_WEIGHTS + (new_w & has_next) * STEP_HAS_NEXT).astype(i32)
    up = jnp.stack([b_s, e_s, f_s, ne_s, nf_s, ob_s, of_s, flags], axis=1).reshape(-1).astype(i32)

    used = blk < n_used
    hb = jnp.minimum(blk, n_used - 1)
    e_b = e_of_blk[hb]
    new_e = used & ((blk == 0) | (e_b != e_b[jnp.maximum(blk - 1, 0)]))
    (ne_b,), has_next_b = _next_flagged(new_e, (e_b,))
    dflags = (used * STEP_VALID + new_e * STEP_NEW_WEIGHTS + (new_e & has_next_b) * STEP_HAS_NEXT).astype(i32)
    down = jnp.stack([hb, e_b, ne_b, dflags], axis=1).reshape(-1).astype(i32)
    free_lo = jnp.concatenate([start_e * tg + cnt, (n_used * tg)[None]])
    free_hi = jnp.concatenate([end_e * tg, jnp.full((1,), nb * tg, i32)])
    free_rows = jnp.stack([free_lo, free_hi], axis=1).reshape(-1).astype(i32)
    return pos, free_rows, dict(up=up, down=down), nb


def _layer(xs, ps, states, w, alpha):
    d_model = xs[0].shape[-1]
    d_rnn = w['w_conv'].shape[1]
    dk, dv = states[0][2].shape[-2:]
    ret_qk = RET_HEADS * dk
    shapes = [x.shape[:2] for x in xs]
    n_rows = [b * l for b, l in shapes]
    row0 = [0, n_rows[0]]
    n_tok = sum(n_rows)

    x2d = [x.reshape(-1, d_model) for x in xs]
    p_all = jnp.concatenate([p.reshape(-1, p.shape[-1]) for p in ps], axis=0).astype(BF16)
    z = _matmul(_stack_cast(x2d, BF16), w['w_in'])

    q_col = 2 * d_rnn // ret_qk
    v_col = (2 * d_rnn + 2 * ret_qk) // (RET_HEADS * dv)
    ret_cols = (q_col, q_col + 1, v_col, v_col + 1)
    assert 2 * d_rnn % ret_qk == 0 and (2 * d_rnn + 2 * ret_qk) % (RET_HEADS * dv) == 0

    ya, yb, new_states = [], [], []
    for gi, ((bsz, seq_len), (conv0, h0, s0)) in enumerate(zip(shapes, states)):
        y_a, conv_n, h_n = _rglru(z, conv0, h0.reshape(bsz, 1, d_rnn), row0[gi], bsz, seq_len,
                                  w['w_conv'], w['b_conv'], w['w_rgate'], w['b_rgate'], w['w_igate'],
                                  w['b_igate'], w['lru_lambda'])
        start = 0 if gi == 0 else PAST_LEN
        cosf, sinf = _rope_tables(start + jnp.arange(seq_len, dtype=jnp.int32), dk)
        y_b, s_n = _retention(z, s0, cosf, sinf, row0[gi], bsz, seq_len, ret_cols)
        ya.append(y_a)
        yb.append(y_b)
        new_states.append((conv_n, h_n.reshape(bsz, d_rnn), s_n))

    pre1 = _outproj(ya, yb, w['w_out'], x2d, alpha)
    h1, h1b, logits = _ln_router(pre1, w['ln1_g'], w['ln1_b'], w['w_router'])
    route, counts = _route(logits)

    tg = 256
    d_exp = w['w_gate'].shape[2]
    tf = _pick(d_exp, (512, 256, 128))
    pos, free_rows, plan, n_blocks = _moe_plan(route, counts, n_tok, tg, d_exp // tf)
    xs_sorted = _dispatch(pos, free_rows, h1, n_blocks * tg)
    y_sorted = _experts(plan, xs_sorted, w['w_gate'], w['w_up'], w['w_down'], tg, tf)
    pre2 = _ple(h1b, w['w_ple_gate'], p_all, w['w_ple_proj'], h1, alpha)
    h2 = _combine(pos, pre2, route, w['ln2_g'], w['ln2_b'], y_sorted, n_rows)

    outs = [h.reshape(b, l, d_model) for h, (b, l) in zip(h2, shapes)]
    return outs, new_states


def kernel(x_prompt, x_sample, state_rglru_conv, state_rglru_h, state_retention, p_prompt, p_sample, w_in, w_conv, b_conv, w_rgate, b_rgate, w_igate, b_igate, lru_lambda, w_out, ln1_g, ln1_b, w_router_group, w_router_expert, w_gate, w_up, w_down, w_ple_gate, w_ple_proj, ln2_g, ln2_b):
    depth = w_in.shape[0]
    alpha = (2.0 * depth) ** 0.25
    bp = x_prompt.shape[0]
    d_model = x_prompt.shape[-1]
    d_rnn = w_conv.shape[-1]
    xs = [x_prompt, x_sample]
    new = [[], []]
    for i in range(depth):
        router = jnp.concatenate([w_router_group[i], w_router_expert[i].reshape(d_model, N_EXPERTS)], axis=1)
        router = jnp.pad(router, ((0, 0), (0, LANES - router.shape[1])))
        w = dict(w_in=w_in[i].astype(BF16), w_conv=w_conv[i], b_conv=b_conv[i].reshape(1, -1),
                 w_rgate=w_rgate[i].astype(BF16), b_rgate=b_rgate[i].reshape(1, -1),
                 w_igate=w_igate[i].astype(BF16), b_igate=b_igate[i].reshape(1, -1),
                 lru_lambda=lru_lambda[i].reshape(1, -1), w_out=w_out[i].astype(BF16),
                 ln1_g=ln1_g[i].reshape(1, -1), ln1_b=ln1_b[i].reshape(1, -1), w_router=router.astype(BF16),
                 w_gate=w_gate[i], w_up=w_up[i], w_down=w_down[i], w_ple_gate=w_ple_gate[i].astype(BF16),
                 w_ple_proj=w_ple_proj[i].astype(BF16), ln2_g=ln2_g[i].reshape(1, -1),
                 ln2_b=ln2_b[i].reshape(1, -1))
        zero_states = (jnp.zeros((bp, CONV_W - 1, d_rnn), x_prompt.dtype),
                       jnp.zeros((bp, d_rnn), x_prompt.dtype),
                       jnp.zeros((bp,) + state_retention.shape[2:], x_prompt.dtype))
        states = [zero_states, (state_rglru_conv[i], state_rglru_h[i], state_retention[i])]
        xs, st = _layer(xs, [p_prompt[i], p_sample[i]], states, w, alpha)
        for gi in range(2):
            new[gi].append(st[gi])
    stack = lambda gi, k: jnp.stack([s[k] for s in new[gi]])
    return (xs[0], xs[1], stack(0, 0), stack(0, 1), stack(0, 2), stack(1, 0), stack(1, 1), stack(1, 2))
```

```python
import functools

import numpy as np
import jax
import jax.numpy as jnp
from jax import lax
from jax.experimental import pallas as pl
from jax.experimental.pallas import tpu as pltpu

CHUNK = 64
RNN_BLOCKS = 16
CONV_W = 4
LRU_C = 8.0
RET_HEADS = 8
ROPE_BASE = 10000.0
N_GROUPS = 4
EXPERTS_PER_GROUP = 8
N_EXPERTS = N_GROUPS * EXPERTS_PER_GROUP
TOP_K = 2
PAST_LEN = 4096
LN_EPS = 1e-5

LANES = 128
SUBLANES = 8
V7X_VMEM_BYTES = 64 * 1024 * 1024
VMEM_LIMIT = V7X_VMEM_BYTES * 7 // 8

F32 = jnp.float32
BF16 = jnp.bfloat16


def _pick(n, cands):
    for c in cands:
        if n % c == 0:
            return c
    raise ValueError(f"no tile in {cands} divides {n}")


def _params(*sem):
    return pltpu.CompilerParams(dimension_semantics=sem, vmem_limit_bytes=VMEM_LIMIT)


def _dot(a, b):
    return jnp.dot(a, b, preferred_element_type=F32)


def _group_specs(block, npb, col_of):
    prompt = pl.BlockSpec(block, lambda i, *a: (jnp.minimum(i, npb - 1), col_of(*a)))
    sample = pl.BlockSpec(block, lambda i, *a: (jnp.maximum(i - npb, 0), col_of(*a)))
    return prompt, sample


def _stack_cast_kernel(xp_ref, xs_ref, o_ref, *, npb):
    i = pl.program_id(0)

    @pl.when(i < npb)
    def _():
        o_ref[...] = xp_ref[...].astype(o_ref.dtype)

    @pl.when(i >= npb)
    def _():
        o_ref[...] = xs_ref[...].astype(o_ref.dtype)


def _stack_cast(x, dtype):
    rows = [a.shape[0] for a in x]
    k = x[0].shape[1]
    tm = _pick(np.gcd(rows[0], rows[1]), (512, 256, 128))
    npb = rows[0] // tm
    return pl.pallas_call(
        functools.partial(_stack_cast_kernel, npb=npb),
        grid=(sum(rows) // tm,),
        in_specs=[*_group_specs((tm, k), npb, lambda: 0)],
        out_specs=pl.BlockSpec((tm, k), lambda i: (i, 0)),
        out_shape=jax.ShapeDtypeStruct((sum(rows), k), dtype),
        compiler_params=_params("parallel"),
        name="stack_cast",
    )(x[0], x[1])


def _mm_kernel(a_ref, b_ref, o_ref):
    o_ref[...] = _dot(a_ref[...], b_ref[...])


def _matmul(a, b):
    m, k = a.shape
    n = b.shape[1]
    tm = _pick(m, (1024, 512, 256, 128))
    tn = _pick(n, (1024, 512, 256, 128))
    return pl.pallas_call(
        _mm_kernel,
        grid=(m // tm, n // tn),
        in_specs=[pl.BlockSpec((tm, k), lambda i, j: (i, 0)),
                  pl.BlockSpec((k, tn), lambda i, j: (0, j))],
        out_specs=pl.BlockSpec((tm, tn), lambda i, j: (i, j)),
        out_shape=jax.ShapeDtypeStruct((m, n), F32),
        compiler_params=_params("parallel", "arbitrary"),
        name="in_proj",
    )(a, b)


def _rglru_kernel(gate_ref, xr_ref, conv0_ref, h0_ref, wconv_ref, bconv_ref, wr_ref, br_ref, wi_ref, bi_ref,
                  lam_ref, y_ref, conv_out_ref, h_out_ref, xbuf, hc, a_buf, u_buf, h_buf, *, t_rows):
    j = pl.program_id(1)
    halo = CONV_W - 1
    top = 8

    @pl.when(j == 0)
    def _():
        xbuf[top - halo:top, :] = conv0_ref[0]
        hc[...] = h0_ref[0]

    xbuf[top:top + t_rows, :] = xr_ref[...]
    conv_out_ref[0] = xr_ref[t_rows - halo:t_rows, :]
    sp = jax.nn.softplus(-lam_ref[...])

    bw = wr_ref.shape[1]
    for n in range(RNN_BLOCKS):
        cs = slice(n * bw, (n + 1) * bw)
        xc = bconv_ref[:, cs]
        for w in range(CONV_W):
            xc = xc + xbuf[top - halo + w:top - halo + w + t_rows, cs] * wconv_ref[w:w + 1, cs]
        xb = xc.astype(BF16)
        r = jax.nn.sigmoid(_dot(xb, wr_ref[n]) + br_ref[:, cs])
        ig = jax.nn.sigmoid(_dot(xb, wi_ref[n]) + bi_ref[:, cs])
        log_a = -LRU_C * r * sp[:, cs]
        a = jnp.exp(log_a)
        a_buf[:, cs] = a
        u_buf[:, cs] = jnp.sqrt(1.0 - a * a) * (ig * xc)

    xbuf[top - halo:top, :] = xr_ref[t_rows - halo:t_rows, :]

    def step(t, h):
        h = a_buf[pl.ds(t, 1), :] * h + u_buf[pl.ds(t, 1), :]
        h_buf[pl.ds(t, 1), :] = h
        return h

    h_last = lax.fori_loop(0, t_rows, step, hc[...], unroll=8)
    hc[...] = h_last
    h_out_ref[0] = h_last
    y_ref[...] = (jax.nn.gelu(gate_ref[...]) * h_buf[...]).astype(y_ref.dtype)


def _rglru(z, conv0, h0, row0, n_seq, seq_len, wconv, bconv, wr, br, wi, bi, lam):
    d_rnn = wconv.shape[1]
    t_rows = _pick(seq_len, (256, 128, 64))
    nb = seq_len // t_rows
    rb0 = row0 // t_rows
    assert row0 % t_rows == 0

    def rows(col):
        return lambda s, j: (rb0 + s * nb + j, col)

    full = lambda shape: pl.BlockSpec(shape, lambda s, j: (0,) * len(shape))
    kern = functools.partial(_rglru_kernel, t_rows=t_rows)
    return pl.pallas_call(
        kern,
        grid=(n_seq, nb),
        in_specs=[pl.BlockSpec((t_rows, d_rnn), rows(0)),
                  pl.BlockSpec((t_rows, d_rnn), rows(1)),
                  pl.BlockSpec((1, CONV_W - 1, d_rnn), lambda s, j: (s, 0, 0)),
                  pl.BlockSpec((1, 1, d_rnn), lambda s, j: (s, 0, 0)),
                  full(wconv.shape), full(bconv.shape), full(wr.shape), full(br.shape),
                  full(wi.shape), full(bi.shape), full(lam.shape)],
        out_specs=[pl.BlockSpec((t_rows, d_rnn), lambda s, j: (s * nb + j, 0)),
                   pl.BlockSpec((1, CONV_W - 1, d_rnn), lambda s, j: (s, 0, 0)),
                   pl.BlockSpec((1, 1, d_rnn), lambda s, j: (s, 0, 0))],
        out_shape=[jax.ShapeDtypeStruct((n_seq * seq_len, d_rnn), BF16),
                   jax.ShapeDtypeStruct((n_seq, CONV_W - 1, d_rnn), F32),
                   jax.ShapeDtypeStruct((n_seq, 1, d_rnn), F32)],
        scratch_shapes=[pltpu.VMEM((t_rows + 8, d_rnn), F32),
                        pltpu.VMEM((1, d_rnn), F32),
                        pltpu.VMEM((t_rows, d_rnn), F32),
                        pltpu.VMEM((t_rows, d_rnn), F32),
                        pltpu.VMEM((t_rows, d_rnn), F32)],
        compiler_params=_params("arbitrary", "arbitrary"),
        name="rglru",
    )(z, z, conv0, h0, wconv, bconv, wr, br, wi, bi, lam)


RET_CHUNKS_PER_STEP = 8


def _retention_consts(dk):
    log_gamma = np.log1p(-np.exp2(-5.0 - np.arange(RET_HEADS, dtype=np.float32))).astype(np.float32)
    idx = np.arange(CHUNK, dtype=np.float32)
    dmat = np.exp(log_gamma[:, None, None] * np.abs(idx[:, None] - idx[None, :])).astype(np.float32)
    q_dec = np.exp(log_gamma[:, None] * (idx[None, :] + 1.0)).astype(np.float32)
    k_dec = np.exp(log_gamma[:, None] * (CHUNK - 1.0 - idx[None, :])).astype(np.float32)
    s_dec = np.exp(log_gamma * CHUNK).astype(np.float32)
    q_dec = np.broadcast_to(q_dec[:, :, None], (RET_HEADS, CHUNK, LANES))
    k_dec = np.broadcast_to(k_dec[:, :, None], (RET_HEADS, CHUNK, dk))
    return jnp.asarray(dmat), jnp.asarray(q_dec), jnp.asarray(k_dec), jnp.asarray(s_dec)


def _retention_kernel(sdec_ref, q_ref, k_ref, v_ref, g_ref, cos_ref, sin_ref, dmat_ref, qdec_ref, kdec_ref,
                      s0_ref, y_ref, s_out_ref, s_acc, *, dk, dv, chunks):
    c = pl.program_id(1)

    @pl.when(c == 0)
    def _():
        s_acc[...] = s0_ref[0]

    scale = dk ** -0.5

    for ci in range(chunks):
        rows = slice(ci * CHUNK, (ci + 1) * CHUNK)
        cosf = cos_ref[rows, :]
        sinf = sin_ref[rows, :]

        def rope(t):
            return t * cosf + pltpu.roll(t, dk // 2, axis=1) * sinf

        for h in range(RET_HEADS):
            qh = rope(q_ref[rows, h * dk:(h + 1) * dk]) * scale
            kh = rope(k_ref[rows, h * dk:(h + 1) * dk])
            vb = v_ref[rows, h * dv:(h + 1) * dv].astype(BF16)
            qb = qh.astype(BF16)
            kb = kh.astype(BF16)
            scores = lax.dot_general(qb, kb, (((1,), (1,)), ((), ())), preferred_element_type=F32) * dmat_ref[h]
            intra = _dot(scores.astype(BF16), vb)
            s_h = s_acc[h]
            qd = qdec_ref[h]
            cross = _dot(qb, s_h.astype(BF16)) * jnp.concatenate([qd] * (dv // LANES), axis=1)
            kd = (kh * kdec_ref[h]).astype(BF16)
            s_acc[h] = s_h * sdec_ref[h] + lax.dot_general(kd, vb, (((0,), (0,)), ((), ())),
                                                           preferred_element_type=F32)
            o = intra + cross
            mu = jnp.mean(o, axis=-1, keepdims=True)
            oc = o - mu
            var = jnp.mean(oc * oc, axis=-1, keepdims=True)
            o = oc * lax.rsqrt(var + LN_EPS)
            gh = g_ref[rows, h * dv:(h + 1) * dv]
            y_ref[rows, h * dv:(h + 1) * dv] = (jax.nn.silu(gh) * o).astype(y_ref.dtype)

    s_out_ref[0] = s_acc[...]


def _retention(z, s0, cosf, sinf, row0, n_seq, seq_len, cols):
    _, heads, dk, dv = s0.shape
    chunk = min(CHUNK, seq_len)
    assert chunk == CHUNK and heads == RET_HEADS
    chunks = _pick(seq_len // chunk, (RET_CHUNKS_PER_STEP, 1))
    t_rows = chunks * chunk
    nc = seq_len // t_rows
    rb0 = row0 // t_rows
    assert row0 % t_rows == 0
    dmat, q_dec, k_dec, s_dec = _retention_consts(dk)

    def rows(col):
        return lambda s, c: (rb0 + s * nc + c, col)

    full = lambda shape: pl.BlockSpec(shape, lambda s, c: (0,) * len(shape))
    kern = functools.partial(_retention_kernel, dk=dk, dv=dv, chunks=chunks)
    return pl.pallas_call(
        kern,
        grid=(n_seq, nc),
        in_specs=[pl.BlockSpec(memory_space=pltpu.SMEM),
                  pl.BlockSpec((t_rows, heads * dk), rows(cols[0])),
                  pl.BlockSpec((t_rows, heads * dk), rows(cols[1])),
                  pl.BlockSpec((t_rows, heads * dv), rows(cols[2])),
                  pl.BlockSpec((t_rows, heads * dv), rows(cols[3])),
                  pl.BlockSpec((t_rows, dk), lambda s, c: (c, 0)),
                  pl.BlockSpec((t_rows, dk), lambda s, c: (c, 0)),
                  full(dmat.shape), full(q_dec.shape), full(k_dec.shape),
                  pl.BlockSpec((1, heads, dk, dv), lambda s, c: (s, 0, 0, 0))],
        out_specs=[pl.BlockSpec((t_rows, heads * dv), lambda s, c: (s * nc + c, 0)),
                   pl.BlockSpec((1, heads, dk, dv), lambda s, c: (s, 0, 0, 0))],
        out_shape=[jax.ShapeDtypeStruct((n_seq * seq_len, heads * dv), BF16),
                   jax.ShapeDtypeStruct((n_seq, heads, dk, dv), F32)],
        scratch_shapes=[pltpu.VMEM((heads, dk, dv), F32)],
        compiler_params=_params("arbitrary", "arbitrary"),
        name="retention",
    )(s_dec, z, z, z, z, cosf, sinf, dmat, q_dec, k_dec, s0)


def _rope_tables(positions, dk):
    half = dk // 2
    inv = ROPE_BASE ** (-jnp.arange(half, dtype=F32) / half)
    ang = positions.astype(F32)[:, None] * inv[None, :]
    cos, sin = jnp.cos(ang), jnp.sin(ang)
    return jnp.concatenate([cos, cos], axis=1), jnp.concatenate([-sin, sin], axis=1)


def _outproj_kernel(yap_ref, yas_ref, ybp_ref, ybs_ref, w1_ref, w2_ref, xp_ref, xs_ref, o_ref, *, alpha, npb):
    i = pl.program_id(0)

    def emit(ya_ref, yb_ref, x_ref):
        mix = _dot(ya_ref[...], w1_ref[...]) + _dot(yb_ref[...], w2_ref[...])
        o_ref[...] = alpha * x_ref[...] + mix

    @pl.when(i < npb)
    def _():
        emit(yap_ref, ybp_ref, xp_ref)

    @pl.when(i >= npb)
    def _():
        emit(yas_ref, ybs_ref, xs_ref)


def _outproj(ya, yb, w_out, x, alpha):
    rows = [a.shape[0] for a in ya]
    ka, kb = ya[0].shape[1], yb[0].shape[1]
    n = w_out.shape[1]
    tm = _pick(np.gcd(rows[0], rows[1]), (1024, 512, 256, 128))
    tn = _pick(n, (512, 256, 128))
    npb = rows[0] // tm
    assert ka == kb
    zero = lambda j: 0
    col = lambda j: j
    return pl.pallas_call(
        functools.partial(_outproj_kernel, alpha=alpha, npb=npb),
        grid=(sum(rows) // tm, n // tn),
        in_specs=[*_group_specs((tm, ka), npb, zero), *_group_specs((tm, kb), npb, zero),
                  pl.BlockSpec((ka, tn), lambda i, j: (0, j)),
                  pl.BlockSpec((kb, tn), lambda i, j: (1, j)),
                  *_group_specs((tm, tn), npb, col)],
        out_specs=pl.BlockSpec((tm, tn), lambda i, j: (i, j)),
        out_shape=jax.ShapeDtypeStruct((sum(rows), n), F32),
        compiler_params=_params("parallel", "arbitrary"),
        name="out_proj",
    )(ya[0], ya[1], yb[0], yb[1], w_out, w_out, x[0], x[1])


def _layernorm_rows(x, g, b):
    mu = jnp.mean(x, axis=-1, keepdims=True)
    xc = x - mu
    var = jnp.mean(xc * xc, axis=-1, keepdims=True)
    return xc * lax.rsqrt(var + LN_EPS) * g + b


def _ln_router_kernel(pre_ref, g_ref, b_ref, wr_ref, h1_ref, h1b_ref, logit_ref):
    y = _layernorm_rows(pre_ref[...], g_ref[...], b_ref[...])
    h1_ref[...] = y
    yb = y.astype(BF16)
    h1b_ref[...] = yb
    logit_ref[...] = _dot(yb, wr_ref[...])


def _ln_router(pre, g, b, w_router):
    m, d = pre.shape
    tm = _pick(m, (256, 128))
    row = lambda i: (i, 0)
    fixed = lambda i: (0, 0)
    return pl.pallas_call(
        _ln_router_kernel,
        grid=(m // tm,),
        in_specs=[pl.BlockSpec((tm, d), row), pl.BlockSpec((1, d), fixed), pl.BlockSpec((1, d), fixed),
                  pl.BlockSpec((d, LANES), fixed)],
        out_specs=[pl.BlockSpec((tm, d), row), pl.BlockSpec((tm, d), row), pl.BlockSpec((tm, LANES), row)],
        out_shape=[jax.ShapeDtypeStruct((m, d), F32), jax.ShapeDtypeStruct((m, d), BF16),
                   jax.ShapeDtypeStruct((m, LANES), F32)],
        compiler_params=_params("parallel"),
        name="ln1_router",
    )(pre, g, b, w_router)


ROUTE_EID, ROUTE_RANK, ROUTE_COMB = 0, 2, 4


def _route_kernel(logit_ref, route_ref, count_ref, carry):
    i = pl.program_id(0)
    tm = logit_ref.shape[0]

    @pl.when(i == 0)
    def _():
        carry[...] = jnp.zeros_like(carry)

    logits = logit_ref[...]
    lane = lax.broadcasted_iota(jnp.int32, logits.shape, 1)
    neg = -jnp.inf
    big = jnp.int32(2 * LANES)

    def first_argmax(vals):
        top = jnp.max(vals, axis=1, keepdims=True)
        return top, jnp.min(jnp.where(vals == top, lane, big), axis=1, keepdims=True)

    gl = jnp.where(lane < N_GROUPS, logits, neg)
    g_max, g_sel = first_argmax(gl)
    g_prob = 1.0 / jnp.sum(jnp.where(lane < N_GROUPS, jnp.exp(logits - g_max), 0.0), axis=1, keepdims=True)
    lo = N_GROUPS + g_sel * EXPERTS_PER_GROUP
    el = jnp.where((lane >= lo) & (lane < lo + EXPERTS_PER_GROUP), logits, neg)
    v1, i1 = first_argmax(el)
    v2, i2 = first_argmax(jnp.where(lane == i1, neg, el))
    e2 = jnp.exp(v2 - v1)
    p1 = 1.0 / (1.0 + e2)
    p2 = e2 / (1.0 + e2)
    eid1 = i1 - N_GROUPS
    eid2 = i2 - N_GROUPS
    oh1 = lane == eid1
    oh2 = lane == eid2
    cnt = oh1.astype(F32) + oh2.astype(F32)
    r_i = lax.broadcasted_iota(jnp.int32, (tm, tm), 0)
    c_i = lax.broadcasted_iota(jnp.int32, (tm, tm), 1)
    tri = jnp.where(c_i < r_i, 1.0, 0.0).astype(BF16)
    before = _dot(tri, cnt.astype(BF16)) + carry[...]
    rank1 = jnp.sum(jnp.where(oh1, before, 0.0), axis=1, keepdims=True)
    rank2 = jnp.sum(jnp.where(oh2, before, 0.0), axis=1, keepdims=True)
    carry[...] = carry[...] + jnp.sum(cnt, axis=0, keepdims=True)
    count_ref[...] = carry[...]

    rec = jnp.zeros(logits.shape, F32)
    for off, val in ((ROUTE_EID, eid1.astype(F32)), (ROUTE_EID + 1, eid2.astype(F32)),
                     (ROUTE_RANK, rank1), (ROUTE_RANK + 1, rank2),
                     (ROUTE_COMB, g_prob * p1), (ROUTE_COMB + 1, g_prob * p2)):
        rec = jnp.where(lane == off, val, rec)
    route_ref[...] = rec


def _route(logits):
    m = logits.shape[0]
    tm = _pick(m, (256, 128))
    return pl.pallas_call(
        _route_kernel,
        grid=(m // tm,),
        in_specs=[pl.BlockSpec((tm, LANES), lambda i: (i, 0))],
        out_specs=[pl.BlockSpec((tm, LANES), lambda i: (i, 0)), pl.BlockSpec((1, LANES), lambda i: (0, 0))],
        out_shape=[jax.ShapeDtypeStruct((m, LANES), F32), jax.ShapeDtypeStruct((1, LANES), F32)],
        scratch_shapes=[pltpu.VMEM((1, LANES), F32)],
        compiler_params=_params("arbitrary"),
        name="route",
    )(logits)


DISPATCH_SLOTS = 3


def _dispatch_kernel(pos_ref, free_ref, src_hbm, xs_ref, zrow, sbuf, zsem, fetch_sems, row_sems, *, tm, n_ranges):
    i = pl.program_id(0)
    n_steps = pl.num_programs(0)

    @pl.when(i == 0)
    def _():
        zrow[...] = jnp.zeros_like(zrow)

        def zero_copy(p):
            return pltpu.make_async_copy(zrow.at[pl.ds(0, 1)], xs_ref.at[pl.ds(p, 1)], zsem)

        def over_free_rows(fn):
            def one_range(r, carry):
                return lax.fori_loop(free_ref[2 * r], free_ref[2 * r + 1], fn, carry)
            lax.fori_loop(0, n_ranges, one_range, 0)

        def start(p, carry):
            zero_copy(p).start()
            return carry

        def wait(p, carry):
            zero_copy(p).wait()
            return carry

        over_free_rows(start)
        over_free_rows(wait)

    def fetch(blk):
        slot = blk % DISPATCH_SLOTS
        rows = pl.ds(pl.multiple_of(blk * tm, tm), tm)
        return pltpu.make_async_copy(src_hbm.at[rows], sbuf.at[slot], fetch_sems.at[slot])

    def row_copy(blk, r, k):
        slot = blk % DISPATCH_SLOTS
        p = pos_ref[(blk * tm + r) * TOP_K + k]
        return pltpu.make_async_copy(sbuf.at[slot, pl.ds(r, 1)], xs_ref.at[pl.ds(p, 1)], row_sems.at[slot])

    def issue(r, carry):
        for k in range(TOP_K):
            row_copy(i, r, k).start()
        return carry

    def drain(blk):
        def body(r, carry):
            for k in range(TOP_K):
                row_copy(blk, r, k).wait()
            return carry
        lax.fori_loop(0, tm, body, 0, unroll=True)

    @pl.when(i == 0)
    def _():
        fetch(0).start()

    @pl.when(i + 1 < n_steps)
    def _():
        fetch(i + 1).start()

    fetch(i).wait()
    lax.fori_loop(0, tm, issue, 0, unroll=8)

    @pl.when(i > 0)
    def _():
        drain(i - 1)

    @pl.when(i == n_steps - 1)
    def _():
        drain(i)


def _dispatch(pos, free_rows, src, n_rows):
    m, d = src.shape
    tm = _pick(m, (256, 128))
    n_ranges = free_rows.shape[0] // 2
    dma_sems = pltpu.SemaphoreType.DMA((DISPATCH_SLOTS,))
    return pl.pallas_call(
        functools.partial(_dispatch_kernel, tm=tm, n_ranges=n_ranges),
        grid_spec=pltpu.PrefetchScalarGridSpec(
            num_scalar_prefetch=2,
            grid=(m // tm,),
            in_specs=[pl.BlockSpec(memory_space=pl.ANY)],
            out_specs=pl.BlockSpec(memory_space=pl.ANY),
            scratch_shapes=[pltpu.VMEM((SUBLANES, d), src.dtype), pltpu.VMEM((DISPATCH_SLOTS, tm, d), src.dtype),
                            pltpu.SemaphoreType.DMA(()), dma_sems, dma_sems]),
        out_shape=jax.ShapeDtypeStruct((n_rows, d), src.dtype),
        compiler_params=_params("arbitrary"),
        name="dispatch",
    )(pos, free_rows, src)


STEP_VALID, STEP_NEW_WEIGHTS, STEP_HAS_NEXT = 1, 2, 4
(UP_B, UP_E, UP_F, UP_NEXT_E, UP_NEXT_F, UP_OUT_B, UP_OUT_F, UP_FLAGS, UP_FIELDS) = range(9)
(DN_HDN_B, DN_E, DN_NEXT_E, DN_FLAGS, DN_FIELDS) = range(5)


def _expert_up_kernel(tab, xs_ref, wg_hbm, wu_hbm, hdn_ref, stage_g, stage_u, wg_bf, wu_bf, sems, *, tf):
    s = pl.program_id(0)
    field = lambda k: tab[s * UP_FIELDS + k]
    flags = field(UP_FLAGS)

    def weight_copies(e, f):
        cols = pl.ds(pl.multiple_of(f * tf, tf), tf)
        return (pltpu.make_async_copy(wg_hbm.at[e, :, cols], stage_g, sems.at[0]),
                pltpu.make_async_copy(wu_hbm.at[e, :, cols], stage_u, sems.at[1]))

    @pl.when(s == 0)
    def _():
        for c in weight_copies(field(UP_E), field(UP_F)):
            c.start()

    @pl.when((flags & STEP_NEW_WEIGHTS) != 0)
    def _():
        for c in weight_copies(field(UP_E), field(UP_F)):
            c.wait()
        wg_bf[...] = stage_g[...].astype(BF16)
        wu_bf[...] = stage_u[...].astype(BF16)

        @pl.when((flags & STEP_HAS_NEXT) != 0)
        def _():
            for c in weight_copies(field(UP_NEXT_E), field(UP_NEXT_F)):
                c.start()

    @pl.when((flags & STEP_VALID) != 0)
    def _():
        x = xs_ref[...].astype(BF16)
        hg = _dot(x, wg_bf[...])
        hu = _dot(x, wu_bf[...])
        hdn_ref[...] = (jax.nn.silu(hg) * hu).astype(hdn_ref.dtype)

    @pl.when((flags & STEP_VALID) == 0)
    def _():
        hdn_ref[...] = jnp.zeros_like(hdn_ref)


def _expert_down_kernel(tab, hdn_ref, wd_hbm, y_ref, stage, wd_bf, sem):
    b = pl.program_id(0)
    field = lambda k: tab[b * DN_FIELDS + k]
    flags = field(DN_FLAGS)

    def weight_copy(e):
        return pltpu.make_async_copy(wd_hbm.at[e], stage, sem)

    @pl.when(b == 0)
    def _():
        weight_copy(field(DN_E)).start()

    @pl.when((flags & STEP_NEW_WEIGHTS) != 0)
    def _():
        weight_copy(field(DN_E)).wait()
        wd_bf[...] = stage[...].astype(BF16)

        @pl.when((flags & STEP_HAS_NEXT) != 0)
        def _():
            weight_copy(field(DN_NEXT_E)).start()

    @pl.when((flags & STEP_VALID) != 0)
    def _():
        y_ref[...] = _dot(hdn_ref[...], wd_bf[...])

    @pl.when((flags & STEP_VALID) == 0)
    def _():
        y_ref[...] = jnp.zeros_like(y_ref)


def _experts(plan, xs, w_gate, w_up, w_down, tg, tf):
    n_rows, d = xs.shape
    _, _, d_exp = w_gate.shape
    nb = n_rows // tg
    n_steps = nb * (d_exp // tf)
    assert plan['up'].shape[0] == n_steps * UP_FIELDS and plan['down'].shape[0] == nb * DN_FIELDS

    hdn = pl.pallas_call(
        functools.partial(_expert_up_kernel, tf=tf),
        grid_spec=pltpu.PrefetchScalarGridSpec(
            num_scalar_prefetch=1,
            grid=(n_steps,),
            in_specs=[pl.BlockSpec((tg, d), lambda s, t: (t[s * UP_FIELDS + UP_B], 0)),
                      pl.BlockSpec(memory_space=pl.ANY),
                      pl.BlockSpec(memory_space=pl.ANY)],
            out_specs=pl.BlockSpec((tg, tf), lambda s, t: (t[s * UP_FIELDS + UP_OUT_B], t[s * UP_FIELDS + UP_OUT_F])),
            scratch_shapes=[pltpu.VMEM((d, tf), F32), pltpu.VMEM((d, tf), F32),
                            pltpu.VMEM((d, tf), BF16), pltpu.VMEM((d, tf), BF16),
                            pltpu.SemaphoreType.DMA((2,))]),
        out_shape=jax.ShapeDtypeStruct((n_rows, d_exp), BF16),
        compiler_params=_params("arbitrary"),
        name="expert_up",
    )(plan['up'], xs, w_gate, w_up)

    return pl.pallas_call(
        _expert_down_kernel,
        grid_spec=pltpu.PrefetchScalarGridSpec(
            num_scalar_prefetch=1,
            grid=(nb,),
            in_specs=[pl.BlockSpec((tg, d_exp), lambda b, t: (t[b * DN_FIELDS + DN_HDN_B], 0)),
                      pl.BlockSpec(memory_space=pl.ANY)],
            out_specs=pl.BlockSpec((tg, d), lambda b, t: (b, 0)),
            scratch_shapes=[pltpu.VMEM((d_exp, d), F32), pltpu.VMEM((d_exp, d), BF16),
                            pltpu.SemaphoreType.DMA(())]),
        out_shape=jax.ShapeDtypeStruct((n_rows, d), F32),
        compiler_params=_params("arbitrary"),
        name="expert_down",
    )(plan['down'], hdn, w_down)


def _ple_kernel(h1b_ref, wg_ref, p_ref, wp_ref, h1_ref, o_ref, *, alpha):
    gate = jax.nn.sigmoid(_dot(h1b_ref[...], wg_ref[...]))
    o_ref[...] = alpha * h1_ref[...] + gate * _dot(p_ref[...], wp_ref[...])


def _ple(h1b, w_gate, p, w_proj, h1, alpha):
    m, d = h1b.shape
    n = w_gate.shape[1]
    dp = p.shape[1]
    tm = _pick(m, (1024, 512, 256, 128))
    tn = _pick(n, (512, 256, 128))
    return pl.pallas_call(
        functools.partial(_ple_kernel, alpha=alpha),
        grid=(m // tm, n // tn),
        in_specs=[pl.BlockSpec((tm, d), lambda i, j: (i, 0)),
                  pl.BlockSpec((d, tn), lambda i, j: (0, j)),
                  pl.BlockSpec((tm, dp), lambda i, j: (i, 0)),
                  pl.BlockSpec((dp, tn), lambda i, j: (0, j)),
                  pl.BlockSpec((tm, tn), lambda i, j: (i, j))],
        out_specs=pl.BlockSpec((tm, tn), lambda i, j: (i, j)),
        out_shape=jax.ShapeDtypeStruct((m, n), F32),
        compiler_params=_params("parallel", "arbitrary"),
        name="ple",
    )(h1b, w_gate, p, w_proj, h1)


def _combine_kernel(pos_ref, pre_ref, route_ref, g_ref, b_ref, y_hbm, op_ref, os_ref, ybuf, sems, *, tm, npb):
    i = pl.program_id(0)
    n_blocks = pl.num_programs(0)
    slot = i % 2

    def row_copy(blk, buf, r, k):
        p = pos_ref[blk * (tm * TOP_K) + TOP_K * r + k]
        return pltpu.make_async_copy(y_hbm.at[pl.ds(p, 1)], ybuf.at[buf, k, pl.ds(r, 1)], sems.at[buf])

    def gather(blk, buf):
        def issue(r, carry):
            for k in range(TOP_K):
                row_copy(blk, buf, r, k).start()
            return carry
        lax.fori_loop(0, tm, issue, 0, unroll=8)

    @pl.when(i == 0)
    def _():
        gather(0, 0)

    @pl.when(i + 1 < n_blocks)
    def _():
        gather(i + 1, 1 - slot)

    def drain(r, carry):
        for k in range(TOP_K):
            row_copy(i, slot, r, k).wait()
        return carry

    lax.fori_loop(0, tm, drain, 0, unroll=True)

    acc = pre_ref[...]
    for k in range(TOP_K):
        acc = acc + ybuf[slot, k] * route_ref[:, ROUTE_COMB + k:ROUTE_COMB + k + 1]
    res = _layernorm_rows(acc, g_ref[...], b_ref[...])

    @pl.when(i < npb)
    def _():
        op_ref[...] = res

    @pl.when(i >= npb)
    def _():
        os_ref[...] = res


def _combine(pos, pre, route, g, b, y, rows):
    m, d = pre.shape
    tm = _pick(np.gcd(rows[0], rows[1]), (256, 128))
    npb = rows[0] // tm
    out_p, out_s = _group_specs((tm, d), npb, lambda pos: 0)
    return pl.pallas_call(
        functools.partial(_combine_kernel, tm=tm, npb=npb),
        grid_spec=pltpu.PrefetchScalarGridSpec(
            num_scalar_prefetch=1,
            grid=(m // tm,),
            in_specs=[pl.BlockSpec((tm, d), lambda i, pos: (i, 0)),
                      pl.BlockSpec((tm, LANES), lambda i, pos: (i, 0)),
                      pl.BlockSpec((1, d), lambda i, pos: (0, 0)),
                      pl.BlockSpec((1, d), lambda i, pos: (0, 0)),
                      pl.BlockSpec(memory_space=pl.ANY)],
            out_specs=[out_p, out_s],
            scratch_shapes=[pltpu.VMEM((2, TOP_K, tm, d), F32), pltpu.SemaphoreType.DMA((2,))]),
        out_shape=[jax.ShapeDtypeStruct((rows[0], d), F32), jax.ShapeDtypeStruct((rows[1], d), F32)],
        compiler_params=_params("arbitrary"),
        name="combine_ln2",
    )(pos, pre, route, g, b, y)


def _next_flagged(flag, values):
    n = flag.shape[0]
    idx = jnp.arange(n, dtype=jnp.int32)
    at_or_after = lax.cummin(jnp.where(flag, idx, n)[::-1])[::-1]
    nxt = jnp.concatenate([at_or_after[1:], jnp.full((1,), n, jnp.int32)])
    return [v[jnp.minimum(nxt, n - 1)] for v in values], nxt < n


def _moe_plan(route, counts, n_tok, tg, nf):
    i32 = jnp.int32
    eid = route[:, ROUTE_EID:ROUTE_EID + TOP_K].astype(i32)
    rank = route[:, ROUTE_RANK:ROUTE_RANK + TOP_K].astype(i32)
    cnt = counts[0, :N_EXPERTS].astype(i32)
    nblk_e = (cnt + tg - 1) // tg
    end_e = jnp.cumsum(nblk_e)
    start_e = end_e - nblk_e
    pos = (start_e[eid] * tg + rank).reshape(-1)
    nb = -(-(n_tok * TOP_K) // tg) + N_EXPERTS
    n_used = end_e[-1]
    blk = jnp.arange(nb, dtype=i32)
    e_of_blk = jnp.sum(end_e[None, :] <= blk[:, None], axis=1).astype(i32)

    run_start = jnp.concatenate([start_e, n_used[None]])
    run_len = jnp.concatenate([nblk_e, (nb - n_used)[None]])
    s = jnp.arange(nb * nf, dtype=i32)
    e_s = e_of_blk[s // nf]
    r0 = run_start[e_s]
    n = jnp.maximum(run_len[e_s], 1)
    local = s - r0 * nf
    f_s = local // n
    b_s = r0 + local % n
    valid = e_s < N_EXPERTS
    last = n_used * nf - 1
    hold = lambda t: jnp.where(valid, t, t[last])
    ob_s, of_s = b_s, f_s
    b_s, f_s, e_s = hold(b_s), hold(f_s), hold(e_s)
    prev = jnp.maximum(s - 1, 0)
    new_w = valid & ((s == 0) | (e_s != e_s[prev]) | (f_s != f_s[prev]))
    (ne_s, nf_s), has_next = _next_flagged(new_w, (e_s, f_s))
    flags = (valid * STEP_VALID + new_w * STEP_NEW_WEIGHTS + (new_w & has_next) * STEP_HAS_NEXT).astype(i32)
    up = jnp.stack([b_s, e_s, f_s, ne_s, nf_s, ob_s, of_s, flags], axis=1).reshape(-1).astype(i32)

    used = blk < n_used
    hb = jnp.minimum(blk, n_used - 1)
    e_b = e_of_blk[hb]
    new_e = used & ((blk == 0) | (e_b != e_b[jnp.maximum(blk - 1, 0)]))
    (ne_b,), has_next_b = _next_flagged(new_e, (e_b,))
    dflags = (used * STEP_VALID + new_e * STEP_NEW_WEIGHTS + (new_e & has_next_b) * STEP_HAS_NEXT).astype(i32)
    down = jnp.stack([hb, e_b, ne_b, dflags], axis=1).reshape(-1).astype(i32)
    free_lo = jnp.concatenate([start_e * tg + cnt, (n_used * tg)[None]])
    free_hi = jnp.concatenate([end_e * tg, jnp.full((1,), nb * tg, i32)])
    free_rows = jnp.stack([free_lo, free_hi], axis=1).reshape(-1).astype(i32)
    return pos, free_rows, dict(up=up, down=down), nb


def _layer(xs, ps, states, w, alpha):
    d_model = xs[0].shape[-1]
    d_rnn = w['w_conv'].shape[1]
    dk, dv = states[0][2].shape[-2:]
    ret_qk = RET_HEADS * dk
    shapes = [x.shape[:2] for x in xs]
    n_rows = [b * l for b, l in shapes]
    row0 = [0, n_rows[0]]
    n_tok = sum(n_rows)

    x2d = [x.reshape(-1, d_model) for x in xs]
    p_all = jnp.concatenate([p.reshape(-1, p.shape[-1]) for p in ps], axis=0).astype(BF16)
    z = _matmul(_stack_cast(x2d, BF16), w['w_in'])

    q_col = 2 * d_rnn // ret_qk
    v_col = (2 * d_rnn + 2 * ret_qk) // (RET_HEADS * dv)
    ret_cols = (q_col, q_col + 1, v_col, v_col + 1)
    assert 2 * d_rnn % ret_qk == 0 and (2 * d_rnn + 2 * ret_qk) % (RET_HEADS * dv) == 0

    ya, yb, new_states = [], [], []
    for gi, ((bsz, seq_len), (conv0, h0, s0)) in enumerate(zip(shapes, states)):
        y_a, conv_n, h_n = _rglru(z, conv0, h0.reshape(bsz, 1, d_rnn), row0[gi], bsz, seq_len,
                                  w['w_conv'], w['b_conv'], w['w_rgate'], w['b_rgate'], w['w_igate'],
                                  w['b_igate'], w['lru_lambda'])
        start = 0 if gi == 0 else PAST_LEN
        cosf, sinf = _rope_tables(start + jnp.arange(seq_len, dtype=jnp.int32), dk)
        y_b, s_n = _retention(z, s0, cosf, sinf, row0[gi], bsz, seq_len, ret_cols)
        ya.append(y_a)
        yb.append(y_b)
        new_states.append((conv_n, h_n.reshape(bsz, d_rnn), s_n))

    pre1 = _outproj(ya, yb, w['w_out'], x2d, alpha)
    h1, h1b, logits = _ln_router(pre1, w['ln1_g'], w['ln1_b'], w['w_router'])
    route, counts = _route(logits)

    tg = 256
    d_exp = w['w_gate'].shape[2]
    tf = _pick(d_exp, (512, 256, 128))
    pos, free_rows, plan, n_blocks = _moe_plan(route, counts, n_tok, tg, d_exp // tf)
    xs_sorted = _dispatch(pos, free_rows, h1, n_blocks * tg)
    y_sorted = _experts(plan, xs_sorted, w['w_gate'], w['w_up'], w['w_down'], tg, tf)
    pre2 = _ple(h1b, w['w_ple_gate'], p_all, w['w_ple_proj'], h1, alpha)
    h2 = _combine(pos, pre2, route, w['ln2_g'], w['ln2_b'], y_sorted, n_rows)

    outs = [h.reshape(b, l, d_model) for h, (b, l) in zip(h2, shapes)]
    return outs, new_states


def kernel(x_prompt, x_sample, state_rglru_conv, state_rglru_h, state_retention, p_prompt, p_sample, w_in, w_conv, b_conv, w_rgate, b_rgate, w_igate, b_igate, lru_lambda, w_out, ln1_g, ln1_b, w_router_group, w_router_expert, w_gate, w_up, w_down, w_ple_gate, w_ple_proj, ln2_g, ln2_b):
    depth = w_in.shape[0]
    alpha = (2.0 * depth) ** 0.25
    bp = x_prompt.shape[0]
    d_model = x_prompt.shape[-1]
    d_rnn = w_conv.shape[-1]
    xs = [x_prompt, x_sample]
    new = [[], []]
    for i in range(depth):
        router = jnp.concatenate([w_router_group[i], w_router_expert[i].reshape(d_model, N_EXPERTS)], axis=1)
        router = jnp.pad(router, ((0, 0), (0, LANES - router.shape[1])))
        w = dict(w_in=w_in[i].astype(BF16), w_conv=w_conv[i], b_conv=b_conv[i].reshape(1, -1),
                 w_rgate=w_rgate[i].astype(BF16), b_rgate=b_rgate[i].reshape(1, -1),
                 w_igate=w_igate[i].astype(BF16), b_igate=b_igate[i].reshape(1, -1),
                 lru_lambda=lru_lambda[i].reshape(1, -1), w_out=w_out[i].astype(BF16),
                 ln1_g=ln1_g[i].reshape(1, -1), ln1_b=ln1_b[i].reshape(1, -1), w_router=router.astype(BF16),
                 w_gate=w_gate[i], w_up=w_up[i], w_down=w_down[i], w_ple_gate=w_ple_gate[i].astype(BF16),
                 w_ple_proj=w_ple_proj[i].astype(BF16), ln2_g=ln2_g[i].reshape(1, -1),
                 ln2_b=ln2_b[i].reshape(1, -1))
        zero_states = (jnp.zeros((bp, CONV_W - 1, d_rnn), x_prompt.dtype),
                       jnp.zeros((bp, d_rnn), x_prompt.dtype),
                       jnp.zeros((bp,) + state_retention.shape[2:], x_prompt.dtype))
        states = [zero_states, (state_rglru_conv[i], state_rglru_h[i], state_retention[i])]
        xs, st = _layer(xs, [p_prompt[i], p_sample[i]], states, w, alpha)
        for gi in range(2):
            new[gi].append(st[gi])
    stack = lambda gi, k: jnp.stack([s[k] for s in new[gi]])
    return (xs[0], xs[1], stack(0, 0), stack(0, 1), stack(0, 2), stack(1, 0), stack(1, 1), stack(1, 2))
```

```python
import functools

import numpy as np
import jax
import jax.numpy as jnp
from jax import lax
from jax.experimental import pallas as pl
from jax.experimental.pallas import tpu as pltpu

CHUNK = 64
RNN_BLOCKS = 16
CONV_W = 4
LRU_C = 8.0
RET_HEADS = 8
ROPE_BASE = 10000.0
N_GROUPS = 4
EXPERTS_PER_GROUP = 8
N_EXPERTS = N_GROUPS * EXPERTS_PER_GROUP
TOP_K = 2
PAST_LEN = 4096
LN_EPS = 1e-5

LANES = 128
SUBLANES = 8
V7X_VMEM_BYTES = 64 * 1024 * 1024
VMEM_LIMIT = V7X_VMEM_BYTES * 7 // 8

F32 = jnp.float32
BF16 = jnp.bfloat16


def _pick(n, cands):
    for c in cands:
        if n % c == 0:
            return c
    raise ValueError(f"no tile in {cands} divides {n}")


def _params(*sem):
    return pltpu.CompilerParams(dimension_semantics=sem, vmem_limit_bytes=VMEM_LIMIT)


def _dot(a, b):
    return jnp.dot(a, b, preferred_element_type=F32)


def _group_specs(block, npb, col_of):
    prompt = pl.BlockSpec(block, lambda i, *a: (jnp.minimum(i, npb - 1), col_of(*a)))
    sample = pl.BlockSpec(block, lambda i, *a: (jnp.maximum(i - npb, 0), col_of(*a)))
    return prompt, sample


def _stack_cast_kernel(xp_ref, xs_ref, o_ref, *, npb):
    i = pl.program_id(0)

    @pl.when(i < npb)
    def _():
        o_ref[...] = xp_ref[...].astype(o_ref.dtype)

    @pl.when(i >= npb)
    def _():
        o_ref[...] = xs_ref[...].astype(o_ref.dtype)


def _stack_cast(x, dtype):
    rows = [a.shape[0] for a in x]
    k = x[0].shape[1]
    tm = _pick(np.gcd(rows[0], rows[1]), (512, 256, 128))
    npb = rows[0] // tm
    return pl.pallas_call(
        functools.partial(_stack_cast_kernel, npb=npb),
        grid=(sum(rows) // tm,),
        in_specs=[*_group_specs((tm, k), npb, lambda: 0)],
        out_specs=pl.BlockSpec((tm, k), lambda i: (i, 0)),
        out_shape=jax.ShapeDtypeStruct((sum(rows), k), dtype),
        compiler_params=_params("parallel"),
        name="stack_cast",
    )(x[0], x[1])


def _mm_kernel(a_ref, b_ref, o_ref):
    o_ref[...] = _dot(a_ref[...], b_ref[...])


def _matmul(a, b):
    m, k = a.shape
    n = b.shape[1]
    tm = _pick(m, (1024, 512, 256, 128))
    tn = _pick(n, (1024, 512, 256, 128))
    return pl.pallas_call(
        _mm_kernel,
        grid=(m // tm, n // tn),
        in_specs=[pl.BlockSpec((tm, k), lambda i, j: (i, 0)),
                  pl.BlockSpec((k, tn), lambda i, j: (0, j))],
        out_specs=pl.BlockSpec((tm, tn), lambda i, j: (i, j)),
        out_shape=jax.ShapeDtypeStruct((m, n), F32),
        compiler_params=_params("parallel", "arbitrary"),
        name="in_proj",
    )(a, b)


def _rglru_kernel(gate_ref, xr_ref, conv0_ref, h0_ref, wconv_ref, bconv_ref, wr_ref, br_ref, wi_ref, bi_ref,
                  lam_ref, y_ref, conv_out_ref, h_out_ref, xbuf, hc, a_buf, u_buf, h_buf, *, t_rows):
    j = pl.program_id(1)
    halo = CONV_W - 1
    top = SUBLANES

    @pl.when(j == 0)
    def _():
        xbuf[top - halo:top, :] = conv0_ref[0]
        hc[...] = h0_ref[0]

    xbuf[top:top + t_rows, :] = xr_ref[...]
    conv_out_ref[0] = xr_ref[t_rows - halo:t_rows, :]
    sp = jax.nn.softplus(-lam_ref[...])

    bw = wr_ref.shape[1]
    for n in range(RNN_BLOCKS):
        cs = slice(n * bw, (n + 1) * bw)
        xc = bconv_ref[:, cs]
        for w in range(CONV_W):
            xc = xc + xbuf[top - halo + w:top - halo + w + t_rows, cs] * wconv_ref[w:w + 1, cs]
        xb = xc.astype(BF16)
        r = jax.nn.sigmoid(_dot(xb, wr_ref[n]) + br_ref[:, cs])
        ig = jax.nn.sigmoid(_dot(xb, wi_ref[n]) + bi_ref[:, cs])
        log_a = -LRU_C * r * sp[:, cs]
        a = jnp.exp(log_a)
        a_buf[:, cs] = a
        u_buf[:, cs] = jnp.sqrt(1.0 - a * a) * (ig * xc)

    xbuf[top - halo:top, :] = xr_ref[t_rows - halo:t_rows, :]

    def step(t, h):
        h = a_buf[pl.ds(t, 1), :] * h + u_buf[pl.ds(t, 1), :]
        h_buf[pl.ds(t, 1), :] = h
        return h

    h_last = lax.fori_loop(0, t_rows, step, hc[...], unroll=8)
    hc[...] = h_last
    h_out_ref[0] = h_last
    y_ref[...] = (jax.nn.gelu(gate_ref[...]) * h_buf[...]).astype(y_ref.dtype)


def _rglru(z, conv0, h0, row0, n_seq, seq_len, wconv, bconv, wr, br, wi, bi, lam):
    d_rnn = wconv.shape[1]
    t_rows = _pick(seq_len, (256, 128, 64))
    nb = seq_len // t_rows
    rb0 = row0 // t_rows
    assert row0 % t_rows == 0

    def rows(col):
        return lambda s, j: (rb0 + s * nb + j, col)

    full = lambda shape: pl.BlockSpec(shape, lambda s, j: (0,) * len(shape))
    kern = functools.partial(_rglru_kernel, t_rows=t_rows)
    return pl.pallas_call(
        kern,
        grid=(n_seq, nb),
        in_specs=[pl.BlockSpec((t_rows, d_rnn), rows(0)),
                  pl.BlockSpec((t_rows, d_rnn), rows(1)),
                  pl.BlockSpec((1, CONV_W - 1, d_rnn), lambda s, j: (s, 0, 0)),
                  pl.BlockSpec((1, 1, d_rnn), lambda s, j: (s, 0, 0)),
                  full(wconv.shape), full(bconv.shape), full(wr.shape), full(br.shape),
                  full(wi.shape), full(bi.shape), full(lam.shape)],
        out_specs=[pl.BlockSpec((t_rows, d_rnn), lambda s, j: (s * nb + j, 0)),
                   pl.BlockSpec((1, CONV_W - 1, d_rnn), lambda s, j: (s, 0, 0)),
                   pl.BlockSpec((1, 1, d_rnn), lambda s, j: (s, 0, 0))],
        out_shape=[jax.ShapeDtypeStruct((n_seq * seq_len, d_rnn), BF16),
                   jax.ShapeDtypeStruct((n_seq, CONV_W - 1, d_rnn), F32),
                   jax.ShapeDtypeStruct((n_seq, 1, d_rnn), F32)],
        scratch_shapes=[pltpu.VMEM((t_rows + SUBLANES, d_rnn), F32),
                        pltpu.VMEM((1, d_rnn), F32),
                        pltpu.VMEM((t_rows, d_rnn), F32),
                        pltpu.VMEM((t_rows, d_rnn), F32),
                        pltpu.VMEM((t_rows, d_rnn), F32)],
        compiler_params=_params("arbitrary", "arbitrary"),
        name="rglru",
    )(z, z, conv0, h0, wconv, bconv, wr, br, wi, bi, lam)


RET_CHUNKS_PER_STEP = 8


def _retention_consts(dk):
    log_gamma = np.log1p(-np.exp2(-5.0 - np.arange(RET_HEADS, dtype=np.float32))).astype(np.float32)
    idx = np.arange(CHUNK, dtype=np.float32)
    dmat = np.exp(log_gamma[:, None, None] * np.abs(idx[:, None] - idx[None, :])).astype(np.float32)
    q_dec = np.exp(log_gamma[:, None] * (idx[None, :] + 1.0)).astype(np.float32)
    k_dec = np.exp(log_gamma[:, None] * (CHUNK - 1.0 - idx[None, :])).astype(np.float32)
    s_dec = np.exp(log_gamma * CHUNK).astype(np.float32)
    q_dec = np.broadcast_to(q_dec[:, :, None], (RET_HEADS, CHUNK, LANES))
    k_dec = np.broadcast_to(k_dec[:, :, None], (RET_HEADS, CHUNK, dk))
    return jnp.asarray(dmat), jnp.asarray(q_dec), jnp.asarray(k_dec), jnp.asarray(s_dec)


def _retention_kernel(sdec_ref, q_ref, k_ref, v_ref, g_ref, cos_ref, sin_ref, dmat_ref, qdec_ref, kdec_ref,
                      s0_ref, y_ref, s_out_ref, s_acc, *, dk, dv, chunks):
    c = pl.program_id(1)

    @pl.when(c == 0)
    def _():
        s_acc[...] = s0_ref[0]

    scale = dk ** -0.5

    for ci in range(chunks):
        rows = slice(ci * CHUNK, (ci + 1) * CHUNK)
        cosf = cos_ref[rows, :]
        sinf = sin_ref[rows, :]

        def rope(t):
            return t * cosf + pltpu.roll(t, dk // 2, axis=1) * sinf

        for h in range(RET_HEADS):
            qh = rope(q_ref[rows, h * dk:(h + 1) * dk]) * scale
            kh = rope(k_ref[rows, h * dk:(h + 1) * dk])
            vb = v_ref[rows, h * dv:(h + 1) * dv].astype(BF16)
            qb = qh.astype(BF16)
            kb = kh.astype(BF16)
            scores = lax.dot_general(qb, kb, (((1,), (1,)), ((), ())), preferred_element_type=F32) * dmat_ref[h]
            intra = _dot(scores.astype(BF16), vb)
            s_h = s_acc[h]
            qd = qdec_ref[h]
            cross = _dot(qb, s_h.astype(BF16)) * jnp.concatenate([qd] * (dv // LANES), axis=1)
            kd = (kh * kdec_ref[h]).astype(BF16)
            s_acc[h] = s_h * sdec_ref[h] + lax.dot_general(kd, vb, (((0,), (0,)), ((), ())),
                                                           preferred_element_type=F32)
            o = intra + cross
            mu = jnp.mean(o, axis=-1, keepdims=True)
            oc = o - mu
            var = jnp.mean(oc * oc, axis=-1, keepdims=True)
            o = oc * lax.rsqrt(var + LN_EPS)
            gh = g_ref[rows, h * dv:(h + 1) * dv]
            y_ref[rows, h * dv:(h + 1) * dv] = (jax.nn.silu(gh) * o).astype(y_ref.dtype)

    s_out_ref[0] = s_acc[...]


def _retention(z, s0, cosf, sinf, row0, n_seq, seq_len, cols):
    _, heads, dk, dv = s0.shape
    chunk = min(CHUNK, seq_len)
    assert chunk == CHUNK and heads == RET_HEADS
    chunks = _pick(seq_len // chunk, (RET_CHUNKS_PER_STEP, 1))
    t_rows = chunks * chunk
    nc = seq_len // t_rows
    rb0 = row0 // t_rows
    assert row0 % t_rows == 0
    dmat, q_dec, k_dec, s_dec = _retention_consts(dk)

    def rows(col):
        return lambda s, c: (rb0 + s * nc + c, col)

    full = lambda shape: pl.BlockSpec(shape, lambda s, c: (0,) * len(shape))
    kern = functools.partial(_retention_kernel, dk=dk, dv=dv, chunks=chunks)
    return pl.pallas_call(
        kern,
        grid=(n_seq, nc),
        in_specs=[pl.BlockSpec(memory_space=pltpu.SMEM),
                  pl.BlockSpec((t_rows, heads * dk), rows(cols[0])),
                  pl.BlockSpec((t_rows, heads * dk), rows(cols[1])),
                  pl.BlockSpec((t_rows, heads * dv), rows(cols[2])),
                  pl.BlockSpec((t_rows, heads * dv), rows(cols[3])),
                  pl.BlockSpec((t_rows, dk), lambda s, c: (c, 0)),
                  pl.BlockSpec((t_rows, dk), lambda s, c: (c, 0)),
                  full(dmat.shape), full(q_dec.shape), full(k_dec.shape),
                  pl.BlockSpec((1, heads, dk, dv), lambda s, c: (s, 0, 0, 0))],
        out_specs=[pl.BlockSpec((t_rows, heads * dv), lambda s, c: (s * nc + c, 0)),
                   pl.BlockSpec((1, heads, dk, dv), lambda s, c: (s, 0, 0, 0))],
        out_shape=[jax.ShapeDtypeStruct((n_seq * seq_len, heads * dv), BF16),
                   jax.ShapeDtypeStruct((n_seq, heads, dk, dv), F32)],
        scratch_shapes=[pltpu.VMEM((heads, dk, dv), F32)],
        compiler_params=_params("arbitrary", "arbitrary"),
        name="retention",
    )(s_dec, z, z, z, z, cosf, sinf, dmat, q_dec, k_dec, s0)


def _rope_tables(positions, dk):
    half = dk // 2
    inv = ROPE_BASE ** (-jnp.arange(half, dtype=F32) / half)
    ang = positions.astype(F32)[:, None] * inv[None, :]
    cos, sin = jnp.cos(ang), jnp.sin(ang)
    return jnp.concatenate([cos, cos], axis=1), jnp.concatenate([-sin, sin], axis=1)


def _outproj_kernel(yap_ref, yas_ref, ybp_ref, ybs_ref, w1_ref, w2_ref, xp_ref, xs_ref, o_ref, *, alpha, npb):
    i = pl.program_id(0)

    def emit(ya_ref, yb_ref, x_ref):
        mix = _dot(ya_ref[...], w1_ref[...]) + _dot(yb_ref[...], w2_ref[...])
        o_ref[...] = alpha * x_ref[...] + mix

    @pl.when(i < npb)
    def _():
        emit(yap_ref, ybp_ref, xp_ref)

    @pl.when(i >= npb)
    def _():
        emit(yas_ref, ybs_ref, xs_ref)


def _outproj(ya, yb, w_out, x, alpha):
    rows = [a.shape[0] for a in ya]
    ka, kb = ya[0].shape[1], yb[0].shape[1]
    n = w_out.shape[1]
    tm = _pick(np.gcd(rows[0], rows[1]), (1024, 512, 256, 128))
    tn = _pick(n, (512, 256, 128))
    npb = rows[0] // tm
    assert ka == kb
    zero = lambda j: 0
    col = lambda j: j
    return pl.pallas_call(
        functools.partial(_outproj_kernel, alpha=alpha, npb=npb),
        grid=(sum(rows) // tm, n // tn),
        in_specs=[*_group_specs((tm, ka), npb, zero), *_group_specs((tm, kb), npb, zero),
                  pl.BlockSpec((ka, tn), lambda i, j: (0, j)),
                  pl.BlockSpec((kb, tn), lambda i, j: (1, j)),
                  *_group_specs((tm, tn), npb, col)],
        out_specs=pl.BlockSpec((tm, tn), lambda i, j: (i, j)),
        out_shape=jax.ShapeDtypeStruct((sum(rows), n), F32),
        compiler_params=_params("parallel", "arbitrary"),
        name="out_proj",
    )(ya[0], ya[1], yb[0], yb[1], w_out, w_out, x[0], x[1])


def _layernorm_rows(x, g, b):
    mu = jnp.mean(x, axis=-1, keepdims=True)
    xc = x - mu
    var = jnp.mean(xc * xc, axis=-1, keepdims=True)
    return xc * lax.rsqrt(var + LN_EPS) * g + b


def _ln_router_kernel(pre_ref, g_ref, b_ref, wr_ref, h1_ref, h1b_ref, logit_ref):
    y = _layernorm_rows(pre_ref[...], g_ref[...], b_ref[...])
    h1_ref[...] = y
    yb = y.astype(BF16)
    h1b_ref[...] = yb
    logit_ref[...] = _dot(yb, wr_ref[...])


def _ln_router(pre, g, b, w_router):
    m, d = pre.shape
    tm = _pick(m, (256, 128))
    row = lambda i: (i, 0)
    fixed = lambda i: (0, 0)
    return pl.pallas_call(
        _ln_router_kernel,
        grid=(m // tm,),
        in_specs=[pl.BlockSpec((tm, d), row), pl.BlockSpec((1, d), fixed), pl.BlockSpec((1, d), fixed),
                  pl.BlockSpec((d, LANES), fixed)],
        out_specs=[pl.BlockSpec((tm, d), row), pl.BlockSpec((tm, d), row), pl.BlockSpec((tm, LANES), row)],
        out_shape=[jax.ShapeDtypeStruct((m, d), F32), jax.ShapeDtypeStruct((m, d), BF16),
                   jax.ShapeDtypeStruct((m, LANES), F32)],
        compiler_params=_params("parallel"),
        name="ln1_router",
    )(pre, g, b, w_router)


ROUTE_EID, ROUTE_RANK, ROUTE_COMB = 0, 2, 4


def _route_kernel(logit_ref, route_ref, count_ref, carry):
    i = pl.program_id(0)
    tm = logit_ref.shape[0]

    @pl.when(i == 0)
    def _():
        carry[...] = jnp.zeros_like(carry)

    logits = logit_ref[...]
    lane = lax.broadcasted_iota(jnp.int32, logits.shape, 1)
    neg = -jnp.inf
    big = jnp.int32(2 * LANES)

    def first_argmax(vals):
        top = jnp.max(vals, axis=1, keepdims=True)
        return top, jnp.min(jnp.where(vals == top, lane, big), axis=1, keepdims=True)

    gl = jnp.where(lane < N_GROUPS, logits, neg)
    g_max, g_sel = first_argmax(gl)
    g_prob = 1.0 / jnp.sum(jnp.where(lane < N_GROUPS, jnp.exp(logits - g_max), 0.0), axis=1, keepdims=True)
    lo = N_GROUPS + g_sel * EXPERTS_PER_GROUP
    el = jnp.where((lane >= lo) & (lane < lo + EXPERTS_PER_GROUP), logits, neg)
    v1, i1 = first_argmax(el)
    v2, i2 = first_argmax(jnp.where(lane == i1, neg, el))
    e2 = jnp.exp(v2 - v1)
    p1 = 1.0 / (1.0 + e2)
    p2 = e2 / (1.0 + e2)
    eid1 = i1 - N_GROUPS
    eid2 = i2 - N_GROUPS
    oh1 = lane == eid1
    oh2 = lane == eid2
    cnt = oh1.astype(F32) + oh2.astype(F32)
    r_i = lax.broadcasted_iota(jnp.int32, (tm, tm), 0)
    c_i = lax.broadcasted_iota(jnp.int32, (tm, tm), 1)
    tri = jnp.where(c_i < r_i, 1.0, 0.0).astype(BF16)
    before = _dot(tri, cnt.astype(BF16)) + carry[...]
    rank1 = jnp.sum(jnp.where(oh1, before, 0.0), axis=1, keepdims=True)
    rank2 = jnp.sum(jnp.where(oh2, before, 0.0), axis=1, keepdims=True)
    carry[...] = carry[...] + jnp.sum(cnt, axis=0, keepdims=True)
    count_ref[...] = carry[...]

    rec = jnp.zeros(logits.shape, F32)
    for off, val in ((ROUTE_EID, eid1.astype(F32)), (ROUTE_EID + 1, eid2.astype(F32)),
                     (ROUTE_RANK, rank1), (ROUTE_RANK + 1, rank2),
                     (ROUTE_COMB, g_prob * p1), (ROUTE_COMB + 1, g_prob * p2)):
        rec = jnp.where(lane == off, val, rec)
    route_ref[...] = rec


def _route(logits):
    m = logits.shape[0]
    tm = _pick(m, (256, 128))
    return pl.pallas_call(
        _route_kernel,
        grid=(m // tm,),
        in_specs=[pl.BlockSpec((tm, LANES), lambda i: (i, 0))],
        out_specs=[pl.BlockSpec((tm, LANES), lambda i: (i, 0)), pl.BlockSpec((1, LANES), lambda i: (0, 0))],
        out_shape=[jax.ShapeDtypeStruct((m, LANES), F32), jax.ShapeDtypeStruct((1, LANES), F32)],
        scratch_shapes=[pltpu.VMEM((1, LANES), F32)],
        compiler_params=_params("arbitrary"),
        name="route",
    )(logits)


DISPATCH_SLOTS = 3


def _dispatch_kernel(pos_ref, free_ref, src_hbm, xs_ref, zrow, sbuf, zsem, fetch_sems, row_sems, *, tm, n_ranges):
    i = pl.program_id(0)
    n_steps = pl.num_programs(0)

    @pl.when(i == 0)
    def _():
        zrow[...] = jnp.zeros_like(zrow)

        def zero_copy(p):
            return pltpu.make_async_copy(zrow.at[pl.ds(0, 1)], xs_ref.at[pl.ds(p, 1)], zsem)

        def over_free_rows(fn):
            def one_range(r, carry):
                return lax.fori_loop(free_ref[2 * r], free_ref[2 * r + 1], fn, carry)
            lax.fori_loop(0, n_ranges, one_range, 0)

        def start(p, carry):
            zero_copy(p).start()
            return carry

        def wait(p, carry):
            zero_copy(p).wait()
            return carry

        over_free_rows(start)
        over_free_rows(wait)

    def fetch(blk):
        slot = blk % DISPATCH_SLOTS
        rows = pl.ds(pl.multiple_of(blk * tm, tm), tm)
        return pltpu.make_async_copy(src_hbm.at[rows], sbuf.at[slot], fetch_sems.at[slot])

    def row_copy(blk, r, k):
        slot = blk % DISPATCH_SLOTS
        p = pos_ref[(blk * tm + r) * TOP_K + k]
        return pltpu.make_async_copy(sbuf.at[slot, pl.ds(r, 1)], xs_ref.at[pl.ds(p, 1)], row_sems.at[slot])

    def issue(r, carry):
        for k in range(TOP_K):
            row_copy(i, r, k).start()
        return carry

    def drain(blk):
        def body(r, carry):
            for k in range(TOP_K):
                row_copy(blk, r, k).wait()
            return carry
        lax.fori_loop(0, tm, body, 0, unroll=True)

    @pl.when(i == 0)
    def _():
        fetch(0).start()

    @pl.when(i + 1 < n_steps)
    def _():
        fetch(i + 1).start()

    fetch(i).wait()
    lax.fori_loop(0, tm, issue, 0, unroll=8)

    @pl.when(i > 0)
    def _():
        drain(i - 1)

    @pl.when(i == n_steps - 1)
    def _():
        drain(i)


def _dispatch(pos, free_rows, src, n_rows):
    m, d = src.shape
    tm = _pick(m, (256, 128))
    n_ranges = free_rows.shape[0] // 2
    dma_sems = pltpu.SemaphoreType.DMA((DISPATCH_SLOTS,))
    return pl.pallas_call(
        functools.partial(_dispatch_kernel, tm=tm, n_ranges=n_ranges),
        grid_spec=pltpu.PrefetchScalarGridSpec(
            num_scalar_prefetch=2,
            grid=(m // tm,),
            in_specs=[pl.BlockSpec(memory_space=pl.ANY)],
            out_specs=pl.BlockSpec(memory_space=pl.ANY),
            scratch_shapes=[pltpu.VMEM((SUBLANES, d), src.dtype), pltpu.VMEM((DISPATCH_SLOTS, tm, d), src.dtype),
                            pltpu.SemaphoreType.DMA(()), dma_sems, dma_sems]),
        out_shape=jax.ShapeDtypeStruct((n_rows, d), src.dtype),
        compiler_params=_params("arbitrary"),
        name="dispatch",
    )(pos, free_rows, src)


STEP_VALID, STEP_NEW_WEIGHTS, STEP_HAS_NEXT = 1, 2, 4
(UP_B, UP_E, UP_F, UP_NEXT_E, UP_NEXT_F, UP_OUT_B, UP_OUT_F, UP_FLAGS, UP_FIELDS) = range(9)
(DN_HDN_B, DN_E, DN_NEXT_E, DN_FLAGS, DN_FIELDS) = range(5)


def _expert_up_kernel(tab, xs_ref, wg_hbm, wu_hbm, hdn_ref, stage_g, stage_u, wg_bf, wu_bf, sems, *, tf):
    s = pl.program_id(0)
    field = lambda k: tab[s * UP_FIELDS + k]
    flags = field(UP_FLAGS)

    def weight_copies(e, f):
        cols = pl.ds(pl.multiple_of(f * tf, tf), tf)
        return (pltpu.make_async_copy(wg_hbm.at[e, :, cols], stage_g, sems.at[0]),
                pltpu.make_async_copy(wu_hbm.at[e, :, cols], stage_u, sems.at[1]))

    @pl.when(s == 0)
    def _():
        for c in weight_copies(field(UP_E), field(UP_F)):
            c.start()

    @pl.when((flags & STEP_NEW_WEIGHTS) != 0)
    def _():
        for c in weight_copies(field(UP_E), field(UP_F)):
            c.wait()
        wg_bf[...] = stage_g[...].astype(BF16)
        wu_bf[...] = stage_u[...].astype(BF16)

        @pl.when((flags & STEP_HAS_NEXT) != 0)
        def _():
            for c in weight_copies(field(UP_NEXT_E), field(UP_NEXT_F)):
                c.start()

    @pl.when((flags & STEP_VALID) != 0)
    def _():
        x = xs_ref[...].astype(BF16)
        hg = _dot(x, wg_bf[...])
        hu = _dot(x, wu_bf[...])
        hdn_ref[...] = (jax.nn.silu(hg) * hu).astype(hdn_ref.dtype)

    @pl.when((flags & STEP_VALID) == 0)
    def _():
        hdn_ref[...] = jnp.zeros_like(hdn_ref)


def _expert_down_kernel(tab, hdn_ref, wd_hbm, y_ref, stage, wd_bf, sem):
    b = pl.program_id(0)
    field = lambda k: tab[b * DN_FIELDS + k]
    flags = field(DN_FLAGS)

    def weight_copy(e):
        return pltpu.make_async_copy(wd_hbm.at[e], stage, sem)

    @pl.when(b == 0)
    def _():
        weight_copy(field(DN_E)).start()

    @pl.when((flags & STEP_NEW_WEIGHTS) != 0)
    def _():
        weight_copy(field(DN_E)).wait()
        wd_bf[...] = stage[...].astype(BF16)

        @pl.when((flags & STEP_HAS_NEXT) != 0)
        def _():
            weight_copy(field(DN_NEXT_E)).start()

    @pl.when((flags & STEP_VALID) != 0)
    def _():
        y_ref[...] = _dot(hdn_ref[...], wd_bf[...])

    @pl.when((flags & STEP_VALID) == 0)
    def _():
        y_ref[...] = jnp.zeros_like(y_ref)


def _experts(plan, xs, w_gate, w_up, w_down, tg, tf):
    n_rows, d = xs.shape
    _, _, d_exp = w_gate.shape
    nb = n_rows // tg
    n_steps = nb * (d_exp // tf)
    assert plan['up'].shape[0] == n_steps * UP_FIELDS and plan['down'].shape[0] == nb * DN_FIELDS

    hdn = pl.pallas_call(
        functools.partial(_expert_up_kernel, tf=tf),
        grid_spec=pltpu.PrefetchScalarGridSpec(
            num_scalar_prefetch=1,
            grid=(n_steps,),
            in_specs=[pl.BlockSpec((tg, d), lambda s, t: (t[s * UP_FIELDS + UP_B], 0)),
                      pl.BlockSpec(memory_space=pl.ANY),
                      pl.BlockSpec(memory_space=pl.ANY)],
            out_specs=pl.BlockSpec((tg, tf), lambda s, t: (t[s * UP_FIELDS + UP_OUT_B], t[s * UP_FIELDS + UP_OUT_F])),
            scratch_shapes=[pltpu.VMEM((d, tf), F32), pltpu.VMEM((d, tf), F32),
                            pltpu.VMEM((d, tf), BF16), pltpu.VMEM((d, tf), BF16),
                            pltpu.SemaphoreType.DMA((2,))]),
        out_shape=jax.ShapeDtypeStruct((n_rows, d_exp), BF16),
        compiler_params=_params("arbitrary"),
        name="expert_up",
    )(plan['up'], xs, w_gate, w_up)

    return pl.pallas_call(
        _expert_down_kernel,
        grid_spec=pltpu.PrefetchScalarGridSpec(
            num_scalar_prefetch=1,
            grid=(nb,),
            in_specs=[pl.BlockSpec((tg, d_exp), lambda b, t: (t[b * DN_FIELDS + DN_HDN_B], 0)),
                      pl.BlockSpec(memory_space=pl.ANY)],
            out_specs=pl.BlockSpec((tg, d), lambda b, t: (b, 0)),
            scratch_shapes=[pltpu.VMEM((d_exp, d), F32), pltpu.VMEM((d_exp, d), BF16),
                            pltpu.SemaphoreType.DMA(())]),
        out_shape=jax.ShapeDtypeStruct((n_rows, d), F32),
        compiler_params=_params("arbitrary"),
        name="expert_down",
    )(plan['down'], hdn, w_down)


def _ple_kernel(h1b_ref, wg_ref, p_ref, wp_ref, h1_ref, o_ref, *, alpha):
    half = o_ref.shape[1] // 2
    for cols in (slice(0, half), slice(half, 2 * half)):
        gate = jax.nn.sigmoid(_dot(h1b_ref[...], wg_ref[:, cols]))
        o_ref[:, cols] = alpha * h1_ref[:, cols] + gate * _dot(p_ref[...], wp_ref[:, cols])


def _ple(h1b, w_gate, p, w_proj, h1, alpha):
    m, d = h1b.shape
    n = w_gate.shape[1]
    dp = p.shape[1]
    tm = _pick(m, (1024, 512, 256, 128))
    tn = _pick(n, (512, 256, 128))
    return pl.pallas_call(
        functools.partial(_ple_kernel, alpha=alpha),
        grid=(m // tm, n // tn),
        in_specs=[pl.BlockSpec((tm, d), lambda i, j: (i, 0)),
                  pl.BlockSpec((d, tn), lambda i, j: (0, j)),
                  pl.BlockSpec((tm, dp), lambda i, j: (i, 0)),
                  pl.BlockSpec((dp, tn), lambda i, j: (0, j)),
                  pl.BlockSpec((tm, tn), lambda i, j: (i, j))],
        out_specs=pl.BlockSpec((tm, tn), lambda i, j: (i, j)),
        out_shape=jax.ShapeDtypeStruct((m, n), F32),
        compiler_params=_params("parallel", "arbitrary"),
        name="ple",
    )(h1b, w_gate, p, w_proj, h1)


def _combine_kernel(pos_ref, pre_ref, route_ref, g_ref, b_ref, y_hbm, op_ref, os_ref, ybuf, sems, *, tm, npb):
    i = pl.program_id(0)
    n_blocks = pl.num_programs(0)
    slot = i % 2

    def row_copy(blk, buf, r, k):
        p = pos_ref[blk * (tm * TOP_K) + TOP_K * r + k]
        return pltpu.make_async_copy(y_hbm.at[pl.ds(p, 1)], ybuf.at[buf, k, pl.ds(r, 1)], sems.at[buf])

    def gather(blk, buf):
        def issue(r, carry):
            for k in range(TOP_K):
                row_copy(blk, buf, r, k).start()
            return carry
        lax.fori_loop(0, tm, issue, 0, unroll=8)

    @pl.when(i == 0)
    def _():
        gather(0, 0)

    @pl.when(i + 1 < n_blocks)
    def _():
        gather(i + 1, 1 - slot)

    def drain(r, carry):
        for k in range(TOP_K):
            row_copy(i, slot, r, k).wait()
        return carry

    lax.fori_loop(0, tm, drain, 0, unroll=True)

    acc = pre_ref[...]
    for k in range(TOP_K):
        acc = acc + ybuf[slot, k] * route_ref[:, ROUTE_COMB + k:ROUTE_COMB + k + 1]
    res = _layernorm_rows(acc, g_ref[...], b_ref[...])

    @pl.when(i < npb)
    def _():
        op_ref[...] = res

    @pl.when(i >= npb)
    def _():
        os_ref[...] = res


def _combine(pos, pre, route, g, b, y, rows):
    m, d = pre.shape
    tm = _pick(np.gcd(rows[0], rows[1]), (256, 128))
    npb = rows[0] // tm
    out_p, out_s = _group_specs((tm, d), npb, lambda pos: 0)
    return pl.pallas_call(
        functools.partial(_combine_kernel, tm=tm, npb=npb),
        grid_spec=pltpu.PrefetchScalarGridSpec(
            num_scalar_prefetch=1,
            grid=(m // tm,),
            in_specs=[pl.BlockSpec((tm, d), lambda i, pos: (i, 0)),
                      pl.BlockSpec((tm, LANES), lambda i, pos: (i, 0)),
                      pl.BlockSpec((1, d), lambda i, pos: (0, 0)),
                      pl.BlockSpec((1, d), lambda i, pos: (0, 0)),
                      pl.BlockSpec(memory_space=pl.ANY)],
            out_specs=[out_p, out_s],
            scratch_shapes=[pltpu.VMEM((2, TOP_K, tm, d), F32), pltpu.SemaphoreType.DMA((2,))]),
        out_shape=[jax.ShapeDtypeStruct((rows[0], d), F32), jax.ShapeDtypeStruct((rows[1], d), F32)],
        compiler_params=_params("arbitrary"),
        name="combine_ln2",
    )(pos, pre, route, g, b, y)


def _next_flagged(flag, values):
    n = flag.shape[0]
    idx = jnp.arange(n, dtype=jnp.int32)
    at_or_after = lax.cummin(jnp.where(flag, idx, n)[::-1])[::-1]
    nxt = jnp.concatenate([at_or_after[1:], jnp.full((1,), n, jnp.int32)])
    return [v[jnp.minimum(nxt, n - 1)] for v in values], nxt < n


def _moe_plan(route, counts, n_tok, tg, nf):
    i32 = jnp.int32
    eid = route[:, ROUTE_EID:ROUTE_EID + TOP_K].astype(i32)
    rank = route[:, ROUTE_RANK:ROUTE_RANK + TOP_K].astype(i32)
    cnt = counts[0, :N_EXPERTS].astype(i32)
    nblk_e = (cnt + tg - 1) // tg
    end_e = jnp.cumsum(nblk_e)
    start_e = end_e - nblk_e
    pos = (start_e[eid] * tg + rank).reshape(-1)
    nb = -(-(n_tok * TOP_K) // tg) + N_EXPERTS
    n_used = end_e[-1]
    blk = jnp.arange(nb, dtype=i32)
    e_of_blk = jnp.sum(end_e[None, :] <= blk[:, None], axis=1).astype(i32)

    run_start = jnp.concatenate([start_e, n_used[None]])
    run_len = jnp.concatenate([nblk_e, (nb - n_used)[None]])
    s = jnp.arange(nb * nf, dtype=i32)
    e_s = e_of_blk[s // nf]
    r0 = run_start[e_s]
    n = jnp.maximum(run_len[e_s], 1)
    local = s - r0 * nf
    f_s = local // n
    b_s = r0 + local % n
    valid = e_s < N_EXPERTS
    last = n_used * nf - 1
    hold = lambda t: jnp.where(valid, t, t[last])
    ob_s, of_s = b_s, f_s
    b_s, f_s, e_s = hold(b_s), hold(f_s), hold(e_s)
    prev = jnp.maximum(s - 1, 0)
    new_w = valid & ((s == 0) | (e_s != e_s[prev]) | (f_s != f_s[prev]))
    (ne_s, nf_s), has_next = _next_flagged(new_w, (e_s, f_s))
    flags = (valid * STEP_VALID + new_w * STEP_NEW_WEIGHTS + (new_w & has_next) * STEP_HAS_NEXT).astype(i32)
    up = jnp.stack([b_s, e_s, f_s, ne_s, nf_s, ob_s, of_s, flags], axis=1).reshape(-1).astype(i32)

    used = blk < n_used
    hb = jnp.minimum(blk, n_used - 1)
    e_b = e_of_blk[hb]
    new_e = used & ((blk == 0) | (e_b != e_b[jnp.maximum(blk - 1, 0)]))
    (ne_b,), has_next_b = _next_flagged(new_e, (e_b,))
    dflags = (used * STEP_VALID + new_e * STEP_NEW_WEIGHTS + (new_e & has_next_b) * STEP_HAS_NEXT).astype(i32)
    down = jnp.stack([hb, e_b, ne_b, dflags], axis=1).reshape(-1).astype(i32)
    free_lo = jnp.concatenate([start_e * tg + cnt, (n_used * tg)[None]])
    free_hi = jnp.concatenate([end_e * tg, jnp.full((1,), nb * tg, i32)])
    free_rows = jnp.stack([free_lo, free_hi], axis=1).reshape(-1).astype(i32)
    return pos, free_rows, dict(up=up, down=down), nb


def _layer(xs, ps, states, w, alpha):
    d_model = xs[0].shape[-1]
    d_rnn = w['w_conv'].shape[1]
    dk, dv = states[0][2].shape[-2:]
    ret_qk = RET_HEADS * dk
    shapes = [x.shape[:2] for x in xs]
    n_rows = [b * l for b, l in shapes]
    row0 = [0, n_rows[0]]
    n_tok = sum(n_rows)

    x2d = [x.reshape(-1, d_model) for x in xs]
    p_all = jnp.concatenate([p.reshape(-1, p.shape[-1]) for p in ps], axis=0).astype(BF16)
    z = _matmul(_stack_cast(x2d, BF16), w['w_in'])

    q_col = 2 * d_rnn // ret_qk
    v_col = (2 * d_rnn + 2 * ret_qk) // (RET_HEADS * dv)
    ret_cols = (q_col, q_col + 1, v_col, v_col + 1)
    assert 2 * d_rnn % ret_qk == 0 and (2 * d_rnn + 2 * ret_qk) % (RET_HEADS * dv) == 0

    ya, yb, new_states = [], [], []
    for gi, ((bsz, seq_len), (conv0, h0, s0)) in enumerate(zip(shapes, states)):
        y_a, conv_n, h_n = _rglru(z, conv0, h0.reshape(bsz, 1, d_rnn), row0[gi], bsz, seq_len,
                                  w['w_conv'], w['b_conv'], w['w_rgate'], w['b_rgate'], w['w_igate'],
                                  w['b_igate'], w['lru_lambda'])
        start = 0 if gi == 0 else PAST_LEN
        cosf, sinf = _rope_tables(start + jnp.arange(seq_len, dtype=jnp.int32), dk)
        y_b, s_n = _retention(z, s0, cosf, sinf, row0[gi], bsz, seq_len, ret_cols)
        ya.append(y_a)
        yb.append(y_b)
        new_states.append((conv_n, h_n.reshape(bsz, d_rnn), s_n))

    pre1 = _outproj(ya, yb, w['w_out'], x2d, alpha)
    h1, h1b, logits = _ln_router(pre1, w['ln1_g'], w['ln1_b'], w['w_router'])
    route, counts = _route(logits)

    tg = 256
    d_exp = w['w_gate'].shape[2]
    tf = _pick(d_exp, (512, 256, 128))
    pos, free_rows, plan, n_blocks = _moe_plan(route, counts, n_tok, tg, d_exp // tf)
    xs_sorted = _dispatch(pos, free_rows, h1, n_blocks * tg)
    y_sorted = _experts(plan, xs_sorted, w['w_gate'], w['w_up'], w['w_down'], tg, tf)
    pre2 = _ple(h1b, w['w_ple_gate'], p_all, w['w_ple_proj'], h1, alpha)
    h2 = _combine(pos, pre2, route, w['ln2_g'], w['ln2_b'], y_sorted, n_rows)

    outs = [h.reshape(b, l, d_model) for h, (b, l) in zip(h2, shapes)]
    return outs, new_states


def kernel(x_prompt, x_sample, state_rglru_conv, state_rglru_h, state_retention, p_prompt, p_sample, w_in, w_conv, b_conv, w_rgate, b_rgate, w_igate, b_igate, lru_lambda, w_out, ln1_g, ln1_b, w_router_group, w_router_expert, w_gate, w_up, w_down, w_ple_gate, w_ple_proj, ln2_g, ln2_b):
    depth = w_in.shape[0]
    alpha = (2.0 * depth) ** 0.25
    bp = x_prompt.shape[0]
    d_model = x_prompt.shape[-1]
    d_rnn = w_conv.shape[-1]
    xs = [x_prompt, x_sample]
    new = [[], []]
    for i in range(depth):
        router = jnp.concatenate([w_router_group[i], w_router_expert[i].reshape(d_model, N_EXPERTS)], axis=1)
        router = jnp.pad(router, ((0, 0), (0, LANES - router.shape[1])))
        w = dict(w_in=w_in[i].astype(BF16), w_conv=w_conv[i], b_conv=b_conv[i].reshape(1, -1),
                 w_rgate=w_rgate[i].astype(BF16), b_rgate=b_rgate[i].reshape(1, -1),
                 w_igate=w_igate[i].astype(BF16), b_igate=b_igate[i].reshape(1, -1),
                 lru_lambda=lru_lambda[i].reshape(1, -1), w_out=w_out[i].astype(BF16),
                 ln1_g=ln1_g[i].reshape(1, -1), ln1_b=ln1_b[i].reshape(1, -1), w_router=router.astype(BF16),
                 w_gate=w_gate[i], w_up=w_up[i], w_down=w_down[i], w_ple_gate=w_ple_gate[i].astype(BF16),
                 w_ple_proj=w_ple_proj[i].astype(BF16), ln2_g=ln2_g[i].reshape(1, -1),
                 ln2_b=ln2_b[i].reshape(1, -1))
        zero_states = (jnp.zeros((bp, CONV_W - 1, d_rnn), x_prompt.dtype),
                       jnp.zeros((bp, d_rnn), x_prompt.dtype),
                       jnp.zeros((bp,) + state_retention.shape[2:], x_prompt.dtype))
        states = [zero_states, (state_rglru_conv[i], state_rglru_h[i], state_retention[i])]
        xs, st = _layer(xs, [p_prompt[i], p_sample[i]], states, w, alpha)
        for gi in range(2):
            new[gi].append(st[gi])
    stack = lambda gi, k: jnp.stack([s[k] for s in new[gi]])
    return (xs[0], xs[1], stack(0, 0), stack(0, 1), stack(0, 2), stack(1, 0), stack(1, 1), stack(1, 2))
```
